```python
import jax, jax.numpy as jnp
from jax import lax
import numpy as np

D_MODEL = 1024
BATCH = 16
SEQ = 2048
DEPTH = 1

N_ATT_HEADS = 8
ATT_HEAD_DIM = 128
KV_LORA = 256
IDX_HEADS = 8
IDX_DIM = 64
TOPK_MAX = 256
Q_BLOCK = 128
N_RET_HEADS = 4
RET_QK_DIM = 128
RET_V_DIM = 256
RET_CHUNK = 128
N_GROUPS = 4
EXPERTS_PER_GROUP = 4
N_EXPERTS = N_GROUPS * EXPERTS_PER_GROUP
EXPERT_FF = 256
TOP_K_EXPERTS = 2
EPS = 1e-6

ATT_Q_W = N_ATT_HEADS * ATT_HEAD_DIM
RET_QK_W = N_RET_HEADS * RET_QK_DIM
RET_V_W = N_RET_HEADS * RET_V_DIM
SPLITS = (ATT_Q_W, KV_LORA, IDX_HEADS * IDX_DIM, IDX_DIM, IDX_HEADS,
          RET_QK_W, RET_QK_W, RET_V_W, RET_V_W, D_MODEL, D_MODEL)
IN_WIDTH = sum(SPLITS)

kernel_name = "hybrid_dsa_retention_hmoe_block"


def rms_norm(x, g):
    xf = x.astype(jnp.float32)
    y = xf * lax.rsqrt(jnp.mean(xf * xf, axis=-1, keepdims=True) + EPS)
    return (y * g.astype(jnp.float32)).astype(x.dtype)


def group_norm_heads(x, g):
    xf = x.astype(jnp.float32)
    mu = jnp.mean(xf, axis=-1, keepdims=True)
    var = jnp.mean(jnp.square(xf - mu), axis=-1, keepdims=True)
    return ((xf - mu) * lax.rsqrt(var + EPS) * g.astype(jnp.float32)).astype(x.dtype)


def alibi_slopes(n_heads):
    return 2.0 ** (-8.0 * jnp.arange(1, n_heads + 1, dtype=jnp.float32) / n_heads)


def retention_decays(n_heads):
    return 1.0 - 2.0 ** (-5.0 - jnp.arange(n_heads, dtype=jnp.float32))


def sparse_indexed_attention(q, k, v, q_idx, k_idx, w_idx, slopes):
    B, S, H, Dh = q.shape
    n_sel = min(TOPK_MAX, S // 4)
    nb = S // Q_BLOCK
    key_pos = jnp.arange(S)
    k_idx32 = k_idx.astype(jnp.float32)

    def to_blocks(a):
        return a.reshape((B, nb, Q_BLOCK) + a.shape[2:]).swapaxes(0, 1)

    def one_block(args):
        qb, qib, wb, blk = args
        q_pos = blk * Q_BLOCK + jnp.arange(Q_BLOCK)
        causal = key_pos[None, :] <= q_pos[:, None]
        dots = jnp.einsum('bqhd,bsd->bqhs', qib.astype(jnp.float32), k_idx32) * (IDX_DIM ** -0.5)
        score = jnp.einsum('bqhs,bqh->bqs', jax.nn.relu(dots), wb.astype(jnp.float32))
        score = jnp.where(causal[None], score, -jnp.inf)
        _, sel = lax.top_k(score, n_sel)
        k_sel = jax.vmap(lambda kk, ii: kk[ii])(k, sel)
        v_sel = jax.vmap(lambda vv, ii: vv[ii])(v, sel)
        logits = jnp.einsum('bqhd,bqkd->bhqk', qb, k_sel).astype(jnp.float32) * (Dh ** -0.5)
        dist = (q_pos[None, :, None] - sel).astype(jnp.float32)
        logits = logits - slopes[None, :, None, None] * dist[:, None]
        valid = sel <= q_pos[None, :, None]
        logits = jnp.where(valid[:, None], logits, -jnp.inf)
        p = jax.nn.softmax(logits, axis=-1).astype(v.dtype)
        return jnp.einsum('bhqk,bqkd->bqhd', p, v_sel)

    out = lax.map(one_block, (to_blocks(q), to_blocks(q_idx), to_blocks(w_idx), jnp.arange(nb)))
    return out.swapaxes(0, 1).reshape(B, S, H, Dh)


def retention_chunkwise(q, k, v, gammas):
    B, S, H, Dk = q.shape
    Dv = v.shape[-1]
    C = RET_CHUNK
    n = S // C
    log_g = jnp.log(gammas)
    i = jnp.arange(C, dtype=jnp.float32)
    rel = i[:, None] - i[None, :]
    decay_intra = jnp.where(rel[None] >= 0, jnp.exp(rel[None] * log_g[:, None, None]), 0.0)
    q_decay = jnp.exp((i + 1.0)[None] * log_g[:, None])
    k_decay = jnp.exp((C - 1.0 - i)[None] * log_g[:, None])
    chunk_decay = jnp.exp(C * log_g)

    def chunks(a):
        return a.astype(jnp.float32).reshape(B, n, C, H, a.shape[-1]).transpose(1, 0, 3, 2, 4)

    qc = chunks(q)
    kc = chunks(k) * (Dk ** -0.5)
    vc = chunks(v)

    def step(state, xs):
        qi, ki, vi = xs
        intra = jnp.einsum('bhqd,bhkd->bhqk', qi, ki) * decay_intra[None]
        o = (jnp.einsum('bhqk,bhkv->bhqv', intra, vi)
             + jnp.einsum('bhqd,bhdv->bhqv', qi * q_decay[None, :, :, None], state))
        state = (chunk_decay[None, :, None, None] * state
                 + jnp.einsum('bhkd,bhkv->bhdv', ki * k_decay[None, :, :, None], vi))
        return state, o

    init = jnp.zeros((B, H, Dk, Dv), jnp.float32)
    _, o = lax.scan(step, init, (qc, kc, vc))
    return o.transpose(1, 0, 3, 2, 4).reshape(B, S, H, Dv)


def hierarchical_moe(h, w_group, b_group, w_er, b_er, w_gate, w_up, w_down):
    B, S, D = h.shape
    t = h.reshape(B * S, D)
    g_logits = (t @ w_group).astype(jnp.float32) + b_group.astype(jnp.float32)
    g_prob = jax.nn.softmax(g_logits, axis=-1)
    g_sel = jnp.argmax(g_logits, axis=-1)
    g_w = jnp.take_along_axis(g_prob, g_sel[:, None], axis=1)
    e_logits = ((t @ w_er).astype(jnp.float32) + b_er.astype(jnp.float32)).reshape(-1, N_GROUPS, EXPERTS_PER_GROUP)
    e_in = jnp.take_along_axis(e_logits, g_sel[:, None, None], axis=1)[:, 0]
    top_v, top_i = lax.top_k(e_in, TOP_K_EXPERTS)
    top_p = jax.nn.softmax(top_v, axis=-1) * g_w
    expert_id = g_sel[:, None] * EXPERTS_PER_GROUP + top_i
    comb = jnp.sum(jax.nn.one_hot(expert_id, N_EXPERTS, dtype=jnp.float32) * top_p[..., None], axis=1)
    hg = jnp.einsum('td,edf->tef', t, w_gate)
    hu = jnp.einsum('td,edf->tef', t, w_up)
    act = jax.nn.silu(hg) * hu * comb[..., None].astype(hg.dtype)
    out = jnp.einsum('tef,efd->td', act, w_down)
    return out.reshape(B, S, D).astype(h.dtype)


def setup_inputs(seed: int = 0) -> dict:
    key = jax.random.key(seed)
    ks = jax.random.split(key, 24)
    L = DEPTH
    f32 = jnp.float32

    def nrm(k, shape, fan_in):
        return jax.random.normal(k, shape, f32) * (fan_in ** -0.5)

    def gain(k, shape):
        return 1.0 + 0.02 * jax.random.normal(k, shape, f32)

    return {
        "x": jax.random.normal(ks[0], (BATCH, SEQ, D_MODEL), f32),
        "norm1_g": gain(ks[1], (L, D_MODEL)),
        "w_in": nrm(ks[2], (L, D_MODEL, IN_WIDTH), D_MODEL),
        "kv_norm_g": gain(ks[3], (L, KV_LORA)),
        "w_kv_up": nrm(ks[4], (L, KV_LORA, 2 * ATT_HEAD_DIM), KV_LORA),
        "q_norm_g": gain(ks[5], (L, ATT_HEAD_DIM)),
        "k_norm_g": gain(ks[6], (L, ATT_HEAD_DIM)),
        "idx_k_norm_g": gain(ks[7], (L, IDX_DIM)),
        "ret_norm_g": gain(ks[8], (L, N_RET_HEADS, RET_V_DIM)),
        "w_att_branch": nrm(ks[9], (L, ATT_Q_W, D_MODEL), ATT_Q_W),
        "w_ret_branch": nrm(ks[10], (L, RET_V_W, D_MODEL), RET_V_W),
        "w_out": nrm(ks[11], (L, D_MODEL, D_MODEL), D_MODEL),
        "norm2_g": gain(ks[12], (L, D_MODEL)),
        "w_group_router": nrm(ks[13], (L, D_MODEL, N_GROUPS), D_MODEL),
        "b_group_router": 0.01 * jax.random.normal(ks[14], (L, N_GROUPS), f32),
        "w_expert_router": nrm(ks[15], (L, D_MODEL, N_EXPERTS), D_MODEL),
        "b_expert_router": 0.01 * jax.random.normal(ks[16], (L, N_EXPERTS), f32),
        "w_exp_gate": nrm(ks[17], (L, N_EXPERTS, D_MODEL, EXPERT_FF), D_MODEL),
        "w_exp_up": nrm(ks[18], (L, N_EXPERTS, D_MODEL, EXPERT_FF), D_MODEL),
        "w_exp_down": nrm(ks[19], (L, N_EXPERTS, EXPERT_FF, D_MODEL), EXPERT_FF),
    }


def reference(x, norm1_g, w_in, kv_norm_g, w_kv_up, q_norm_g, k_norm_g, idx_k_norm_g,
              ret_norm_g, w_att_branch, w_ret_branch, w_out, norm2_g, w_group_router,
              b_group_router, w_expert_router, b_expert_router, w_exp_gate, w_exp_up,
              w_exp_down):
    B, S, D = x.shape
    slopes = alibi_slopes(N_ATT_HEADS)
    gammas = retention_decays(N_RET_HEADS)
    points = np.cumsum(np.array(SPLITS))[:-1].tolist()
    h = x
    for l in range(DEPTH):
        n = rms_norm(h, norm1_g[l])
        proj = n @ w_in[l]
        (q_att, c_kv, q_idx, k_idx, w_idx, r_q, r_k, r_v, r_gate,
         g_att, g_ret) = jnp.split(proj, points, axis=-1)

        q = rms_norm(q_att.reshape(B, S, N_ATT_HEADS, ATT_HEAD_DIM), q_norm_g[l])
        kv = rms_norm(c_kv, kv_norm_g[l]) @ w_kv_up[l]
        k_sh, v_sh = jnp.split(kv, 2, axis=-1)
        k_sh = rms_norm(k_sh, k_norm_g[l])
        k_idx = rms_norm(k_idx, idx_k_norm_g[l])
        w_idx = w_idx * (IDX_HEADS ** -0.5)
        att = sparse_indexed_attention(q, k_sh, v_sh,
                                       q_idx.reshape(B, S, IDX_HEADS, IDX_DIM),
                                       k_idx, w_idx, slopes)
        y_att = att.reshape(B, S, ATT_Q_W) @ w_att_branch[l]

        ret = retention_chunkwise(r_q.reshape(B, S, N_RET_HEADS, RET_QK_DIM),
                                  r_k.reshape(B, S, N_RET_HEADS, RET_QK_DIM),
                                  r_v.reshape(B, S, N_RET_HEADS, RET_V_DIM), gammas)
        ret = group_norm_heads(ret, ret_norm_g[l]).astype(h.dtype)
        y_ret = (jax.nn.silu(r_gate) * ret.reshape(B, S, RET_V_W)) @ w_ret_branch[l]

        mixed = jax.nn.sigmoid(g_att) * y_att + jax.nn.sigmoid(g_ret) * y_ret
        h = h + mixed @ w_out[l]

        n2 = rms_norm(h, norm2_g[l])
        h = h + hierarchical_moe(n2, w_group_router[l], b_group_router[l],
                                 w_expert_router[l], b_expert_router[l],
                                 w_exp_gate[l], w_exp_up[l], w_exp_down[l])
    return h
```

```python
import functools
import math

import jax
import jax.numpy as jnp
from jax import lax
from jax.experimental import pallas as pl
from jax.experimental.pallas import tpu as pltpu

D_MODEL = 1024
N_ATT_HEADS = 8
ATT_HEAD_DIM = 128
KV_LORA = 256
IDX_HEADS = 8
IDX_DIM = 64
TOPK_MAX = 256
Q_BLOCK = 128
N_RET_HEADS = 4
RET_QK_DIM = 128
RET_V_DIM = 256
RET_CHUNK = 128
N_GROUPS = 4
EXPERTS_PER_GROUP = 4
N_EXPERTS = N_GROUPS * EXPERTS_PER_GROUP
EXPERT_FF = 256
EPS = 1e-6

LANES = 128
KEY_CHUNK = 256
PROJ_W = 7168
C_QATT, C_QIDX, C_CKV, C_KIDX, C_WIDX = 0, 1024, 1536, 1792, 1856
C_RQ, C_RK, C_RV, C_RGATE, C_GATT, C_GRET = 2048, 2560, 3072, 4096, 5120, 6144
ROUTER_W = 128
NEG_BIG = -1e30
VMEM_LIMIT = 56 * 1024 * 1024

_INT_MIN = -(2 ** 31)


def _cparams(sem):
    return pltpu.CompilerParams(dimension_semantics=sem, vmem_limit_bytes=VMEM_LIMIT)


def _proj_kernel(x_ref, g_ref, w_ref, o_ref, n_ref):
    @pl.when(pl.program_id(1) == 0)
    def _():
        x = x_ref[...]
        ms = jnp.mean(x * x, axis=-1, keepdims=True)
        n_ref[...] = (x * lax.rsqrt(ms + EPS) * g_ref[...]).astype(jnp.bfloat16)

    o_ref[...] = jnp.dot(n_ref[...], w_ref[...],
                         preferred_element_type=jnp.float32).astype(o_ref.dtype)


def _proj(x2, g1, w_p, tm, tn):
    T = x2.shape[0]
    return pl.pallas_call(
        _proj_kernel,
        out_shape=jax.ShapeDtypeStruct((T, PROJ_W), jnp.bfloat16),
        grid=(T // tm, PROJ_W // tn),
        in_specs=[pl.BlockSpec((tm, D_MODEL), lambda i, j: (i, 0)),
                  pl.BlockSpec((1, D_MODEL), lambda i, j: (0, 0)),
                  pl.BlockSpec((D_MODEL, tn), lambda i, j: (0, j))],
        out_specs=pl.BlockSpec((tm, tn), lambda i, j: (i, j)),
        scratch_shapes=[pltpu.VMEM((tm, D_MODEL), jnp.bfloat16)],
        compiler_params=_cparams(("parallel", "arbitrary")),
        name="proj",
    )(x2, g1, w_p)


def _prep_kernel(q_ref, c_ref, qg_ref, kvg_ref, wkv_ref, kg_ref, ig_ref,
                 qn_ref, k_ref, v_ref, klo_ref, khi_ref, w_ref):
    qg = qg_ref[...]
    for h in range(N_ATT_HEADS):
        sl = slice(h * ATT_HEAD_DIM, (h + 1) * ATT_HEAD_DIM)
        qh = q_ref[:, sl].astype(jnp.float32)
        ms = jnp.mean(qh * qh, axis=-1, keepdims=True)
        qn_ref[:, sl] = (qh * lax.rsqrt(ms + EPS) * qg * (ATT_HEAD_DIM ** -0.5)).astype(jnp.bfloat16)

    c = c_ref[:, 0:KV_LORA].astype(jnp.float32)
    ms = jnp.mean(c * c, axis=-1, keepdims=True)
    cn = (c * lax.rsqrt(ms + EPS) * kvg_ref[...]).astype(jnp.bfloat16)
    kv = jnp.dot(cn, wkv_ref[...], preferred_element_type=jnp.float32)
    k = kv[:, :ATT_HEAD_DIM]
    ms = jnp.mean(k * k, axis=-1, keepdims=True)
    k_ref[...] = (k * lax.rsqrt(ms + EPS) * kg_ref[...]).astype(jnp.bfloat16)
    v_ref[...] = kv[:, ATT_HEAD_DIM:].astype(jnp.bfloat16)

    blk = c_ref[:, C_KIDX - C_CKV:C_KIDX - C_CKV + LANES].astype(jnp.float32)
    lane = lax.broadcasted_iota(jnp.int32, blk.shape, 1)
    is_k = lane < IDX_DIM
    ms = jnp.sum(jnp.where(is_k, blk * blk, 0.0), axis=-1, keepdims=True) * (1.0 / IDX_DIM)
    kn = blk * lax.rsqrt(ms + EPS) * ig_ref[...]
    klo_ref[...] = jnp.where(is_k, kn, 0.0).astype(jnp.bfloat16)
    kn_hi = pltpu.roll(kn, IDX_DIM, 1)
    khi_ref[...] = jnp.where(is_k, 0.0, kn_hi).astype(jnp.bfloat16)
    w_ref[...] = blk * ((IDX_HEADS ** -0.5) * (IDX_DIM ** -0.5))


def _prep(proj, q_norm_g, kv_norm_g, w_kv_up, k_norm_g, idx_g_pad, tm):
    T = proj.shape[0]
    row = lambda i: (i, 0)
    const = lambda i: (0, 0)
    return pl.pallas_call(
        _prep_kernel,
        out_shape=(jax.ShapeDtypeStruct((T, D_MODEL), jnp.bfloat16),
                   jax.ShapeDtypeStruct((T, LANES), jnp.bfloat16),
                   jax.ShapeDtypeStruct((T, LANES), jnp.bfloat16),
                   jax.ShapeDtypeStruct((T, LANES), jnp.bfloat16),
                   jax.ShapeDtypeStruct((T, LANES), jnp.bfloat16),
                   jax.ShapeDtypeStruct((T, LANES), jnp.float32)),
        grid=(T // tm,),
        in_specs=[pl.BlockSpec((tm, 1024), lambda i: (i, C_QATT // 1024)),
                  pl.BlockSpec((tm, 512), lambda i: (i, C_CKV // 512)),
                  pl.BlockSpec((1, ATT_HEAD_DIM), const),
                  pl.BlockSpec((1, KV_LORA), const),
                  pl.BlockSpec((KV_LORA, 2 * ATT_HEAD_DIM), const),
                  pl.BlockSpec((1, ATT_HEAD_DIM), const),
                  pl.BlockSpec((1, LANES), const)],
        out_specs=(pl.BlockSpec((tm, D_MODEL), row),
                   pl.BlockSpec((tm, LANES), row),
                   pl.BlockSpec((tm, LANES), row),
                   pl.BlockSpec((tm, LANES), row),
                   pl.BlockSpec((tm, LANES), row),
                   pl.BlockSpec((tm, LANES), row)),
        compiler_params=_cparams(("parallel",)),
        name="prep",
    )(proj, proj, q_norm_g, kv_norm_g, w_kv_up, k_norm_g, idx_g_pad)


def _attn_kernel(n_sel, q_ref, qi_ref, w_ref, klo_ref, khi_ref, k_ref, v_ref, o_ref,
                 sc_ref, m_ref, l_ref, acc_ref):
    j = pl.program_id(1)
    nch = (j * Q_BLOCK + Q_BLOCK + KEY_CHUNK - 1) // KEY_CHUNK
    Q = Q_BLOCK
    q_pos = j * Q + lax.broadcasted_iota(jnp.int32, (Q, KEY_CHUNK), 0)
    lane_pos = lax.broadcasted_iota(jnp.int32, (Q, KEY_CHUNK), 1)

    w_all = w_ref[...]
    w_cols = [w_all[:, IDX_DIM + h:IDX_DIM + h + 1] for h in range(IDX_HEADS)]

    def score_chunk(c, carry):
        r0 = pl.multiple_of(c * KEY_CHUNK, KEY_CHUNK)
        klo = klo_ref[pl.ds(r0, KEY_CHUNK), :]
        khi = khi_ref[pl.ds(r0, KEY_CHUNK), :]
        s = jnp.zeros((Q, KEY_CHUNK), jnp.float32)
        for p in range(IDX_HEADS // 2):
            qp = qi_ref[:, p * LANES:(p + 1) * LANES]
            d0 = lax.dot_general(qp, klo, (((1,), (1,)), ((), ())),
                                 preferred_element_type=jnp.float32)
            d1 = lax.dot_general(qp, khi, (((1,), (1,)), ((), ())),
                                 preferred_element_type=jnp.float32)
            s = s + w_cols[2 * p] * jnp.maximum(d0, 0.0) + w_cols[2 * p + 1] * jnp.maximum(d1, 0.0)
        causal = (c * KEY_CHUNK + lane_pos) <= q_pos
        sc_ref[c] = jnp.where(causal, s, -jnp.inf)
        return carry

    lax.fori_loop(0, nch, score_chunk, 0)

    row_pos = j * Q + lax.broadcasted_iota(jnp.int32, (Q, 1), 0)
    k_row = jnp.minimum(row_pos + 1, n_sel).astype(jnp.float32)

    def count_ge(cand):
        cand_b = jnp.broadcast_to(cand, (Q, LANES))

        def body(c, acc):
            blk = sc_ref[c]
            return (acc + jnp.where(blk[:, :LANES] >= cand_b, 1.0, 0.0)
                    + jnp.where(blk[:, LANES:] >= cand_b, 1.0, 0.0))

        acc = lax.fori_loop(0, nch, body, jnp.zeros((Q, LANES), jnp.float32))
        return jnp.sum(acc, axis=-1, keepdims=True)

    def key_to_float(u):
        key = u ^ _INT_MIN
        bits = jnp.where(key >= 0, key, key ^ 0x7FFFFFFF)
        return lax.bitcast_convert_type(bits, jnp.float32)

    def bit_body(i, carry):
        prefix, cnt_best = carry
        cand_u = prefix | lax.shift_left(jnp.int32(1), 31 - i)
        cnt = count_ge(key_to_float(cand_u))
        take = cnt >= k_row
        return jnp.where(take, cand_u, prefix), jnp.where(take, cnt, cnt_best)

    prefix, cnt_best = lax.fori_loop(
        0, 32, bit_body, (jnp.zeros((Q, 1), jnp.int32), k_row))
    thr = key_to_float(prefix)

    @pl.when(jnp.max(cnt_best - k_row) > 0.0)
    def _():
        thr_b = jnp.broadcast_to(thr, (Q, LANES))

        def gt_body(c, acc):
            blk = sc_ref[c]
            return (acc + jnp.where(blk[:, :LANES] > thr_b, 1.0, 0.0)
                    + jnp.where(blk[:, LANES:] > thr_b, 1.0, 0.0))

        n_gt = jnp.sum(lax.fori_loop(0, nch, gt_body, jnp.zeros((Q, LANES), jnp.float32)),
                       axis=-1, keepdims=True)
        need = k_row - n_gt
        r_i = lax.broadcasted_iota(jnp.int32, (KEY_CHUNK, KEY_CHUNK), 0)
        c_i = lax.broadcasted_iota(jnp.int32, (KEY_CHUNK, KEY_CHUNK), 1)
        tri = jnp.where(r_i <= c_i, 1.0, 0.0).astype(jnp.bfloat16)

        def drop_body(c, seen):
            blk = sc_ref[c]
            eq = blk == thr
            eq_f = jnp.where(eq, 1.0, 0.0)
            incl = jnp.dot(eq_f.astype(jnp.bfloat16), tri, preferred_element_type=jnp.float32)
            rank = seen + incl - eq_f
            sc_ref[c] = jnp.where(eq & (rank >= need), -jnp.inf, blk)
            return seen + jnp.sum(eq_f, axis=-1, keepdims=True)

        lax.fori_loop(0, nch, drop_body, jnp.zeros((Q, 1), jnp.float32))

    m_ref[...] = jnp.full(m_ref.shape, NEG_BIG, jnp.float32)
    l_ref[...] = jnp.zeros(l_ref.shape, jnp.float32)
    acc_ref[...] = jnp.zeros(acc_ref.shape, jnp.float32)

    def att_chunk(c, carry):
        r0 = pl.multiple_of(c * KEY_CHUNK, KEY_CHUNK)
        kc = k_ref[pl.ds(r0, KEY_CHUNK), :]
        vc = v_ref[pl.ds(r0, KEY_CHUNK), :]
        sel = sc_ref[c] >= thr
        dist = (q_pos - (c * KEY_CHUNK + lane_pos)).astype(jnp.float32)
        for h in range(N_ATT_HEADS):
            slope = 2.0 ** (-8.0 * (h + 1) / N_ATT_HEADS)
            qh = q_ref[:, h * ATT_HEAD_DIM:(h + 1) * ATT_HEAD_DIM]
            lg = lax.dot_general(qh, kc, (((1,), (1,)), ((), ())),
                                 preferred_element_type=jnp.float32)
            lg = jnp.where(sel, lg - slope * dist, NEG_BIG)
            m_old = m_ref[h][:, :1]
            m_new = jnp.maximum(m_old, jnp.max(lg, axis=-1, keepdims=True))
            alpha = jnp.exp(m_old - m_new)
            p = jnp.exp(lg - m_new)
            l_ref[h] = jnp.broadcast_to(alpha * l_ref[h][:, :1] + jnp.sum(p, axis=-1, keepdims=True),
                                        (Q, LANES))
            acc_ref[h] = alpha * acc_ref[h] + jnp.dot(p.astype(jnp.bfloat16), vc,
                                                      preferred_element_type=jnp.float32)
            m_ref[h] = jnp.broadcast_to(m_new, (Q, LANES))
        return carry

    lax.fori_loop(0, nch, att_chunk, 0)

    for h in range(N_ATT_HEADS):
        o_ref[:, h * ATT_HEAD_DIM:(h + 1) * ATT_HEAD_DIM] = (
            acc_ref[h] / l_ref[h][:, :1]).astype(o_ref.dtype)


def _attn(qn, proj, widx, klo, khi, k_sh, v_sh, B, S, n_sel):
    T = B * S
    nb = S // Q_BLOCK
    qrow = lambda b, j: (b * nb + j, 0)
    per_b = lambda b, j: (b, 0)
    n_kchunks = (S + KEY_CHUNK - 1) // KEY_CHUNK
    return pl.pallas_call(
        functools.partial(_attn_kernel, n_sel),
        out_shape=jax.ShapeDtypeStruct((T, D_MODEL), jnp.bfloat16),
        grid=(B, nb),
        in_specs=[pl.BlockSpec((Q_BLOCK, D_MODEL), qrow),
                  pl.BlockSpec((Q_BLOCK, 512), lambda b, j: (b * nb + j, C_QIDX // 512)),
                  pl.BlockSpec((Q_BLOCK, LANES), qrow),
                  pl.BlockSpec((S, LANES), per_b),
                  pl.BlockSpec((S, LANES), per_b),
                  pl.BlockSpec((S, LANES), per_b),
                  pl.BlockSpec((S, LANES), per_b)],
        out_specs=pl.BlockSpec((Q_BLOCK, D_MODEL), qrow),
        scratch_shapes=[pltpu.VMEM((n_kchunks, Q_BLOCK, KEY_CHUNK), jnp.float32),
                        pltpu.VMEM((N_ATT_HEADS, Q_BLOCK, LANES), jnp.float32),
                        pltpu.VMEM((N_ATT_HEADS, Q_BLOCK, LANES), jnp.float32),
                        pltpu.VMEM((N_ATT_HEADS, Q_BLOCK, ATT_HEAD_DIM), jnp.float32)],
        compiler_params=_cparams(("parallel", "arbitrary")),
        name="attn",
    )(qn, proj, widx, klo, khi, k_sh, v_sh)


def _ret_kernel(rq_ref, rk_ref, rv_ref, gate_ref, g_ref, o_ref, state_ref):
    C = RET_CHUNK

    @pl.when(pl.program_id(1) == 0)
    def _():
        state_ref[...] = jnp.zeros(state_ref.shape, jnp.float32)

    ii = lax.broadcasted_iota(jnp.int32, (C, C), 0)
    jj = lax.broadcasted_iota(jnp.int32, (C, C), 1)
    rel = (ii - jj).astype(jnp.float32)
    pos = lax.broadcasted_iota(jnp.int32, (C, 1), 0).astype(jnp.float32)
    for h in range(N_RET_HEADS):
        log_g = math.log(1.0 - 2.0 ** (-5.0 - h))
        decay_intra = jnp.where(rel >= 0, jnp.exp(rel * log_g), 0.0)
        q_decay = jnp.exp((pos + 1.0) * log_g)
        k_decay = jnp.exp((C - 1.0 - pos) * log_g)
        chunk_decay = math.exp(C * log_g)

        q = rq_ref[:, h * RET_QK_DIM:(h + 1) * RET_QK_DIM].astype(jnp.float32)
        k = rk_ref[:, h * RET_QK_DIM:(h + 1) * RET_QK_DIM].astype(jnp.float32) * (RET_QK_DIM ** -0.5)
        v = rv_ref[:, h * RET_V_DIM:(h + 1) * RET_V_DIM]
        state = state_ref[h]

        qk = lax.dot_general(q.astype(jnp.bfloat16), k.astype(jnp.bfloat16),
                             (((1,), (1,)), ((), ())), preferred_element_type=jnp.float32)
        intra = (qk * decay_intra).astype(jnp.bfloat16)
        o = (jnp.dot(intra, v, preferred_element_type=jnp.float32)
             + jnp.dot((q * q_decay).astype(jnp.bfloat16), state.astype(jnp.bfloat16),
                       preferred_element_type=jnp.float32))
        kd_t = (k * k_decay).T.astype(jnp.bfloat16)
        state_ref[h] = chunk_decay * state + jnp.dot(kd_t, v, preferred_element_type=jnp.float32)

        mu = jnp.mean(o, axis=-1, keepdims=True)
        var = jnp.mean(jnp.square(o - mu), axis=-1, keepdims=True)
        sl = slice(h * RET_V_DIM, (h + 1) * RET_V_DIM)
        y = (o - mu) * lax.rsqrt(var + EPS) * g_ref[:, sl]
        gate = gate_ref[:, sl].astype(jnp.float32)
        o_ref[:, sl] = (gate * jax.nn.sigmoid(gate) * y).astype(o_ref.dtype)


def _ret(proj, ret_g, B, S):
    T = B * S
    n = S // RET_CHUNK
    C = RET_CHUNK
    return pl.pallas_call(
        _ret_kernel,
        out_shape=jax.ShapeDtypeStruct((T, N_RET_HEADS * RET_V_DIM), jnp.bfloat16),
        grid=(B, n),
        in_specs=[pl.BlockSpec((C, 512), lambda b, i: (b * n + i, C_RQ // 512)),
                  pl.BlockSpec((C, 512), lambda b, i: (b * n + i, C_RK // 512)),
                  pl.BlockSpec((C, 1024), lambda b, i: (b * n + i, C_RV // 1024)),
                  pl.BlockSpec((C, 1024), lambda b, i: (b * n + i, C_RGATE // 1024)),
                  pl.BlockSpec((1, 1024), lambda b, i: (0, 0))],
        out_specs=pl.BlockSpec((C, 1024), lambda b, i: (b * n + i, 0)),
        scratch_shapes=[pltpu.VMEM((N_RET_HEADS, RET_QK_DIM, RET_V_DIM), jnp.float32)],
        compiler_params=_cparams(("parallel", "arbitrary")),
        name="ret",
    )(proj, proj, proj, proj, ret_g)


def _merge_kernel(att_ref, ret_ref, ga_ref, gr_ref, x_ref, wa_ref, wr_ref, wo_ref,
                  g2_ref, wrt_ref, brt_ref, h_ref, n2_ref, comb_ref):
    y_att = jnp.dot(att_ref[...], wa_ref[...], preferred_element_type=jnp.float32)
    y_ret = jnp.dot(ret_ref[...], wr_ref[...], preferred_element_type=jnp.float32)
    mixed = (jax.nn.sigmoid(ga_ref[...].astype(jnp.float32)) * y_att
             + jax.nn.sigmoid(gr_ref[...].astype(jnp.float32)) * y_ret)
    h = x_ref[...] + jnp.dot(mixed.astype(jnp.bfloat16), wo_ref[...],
                             preferred_element_type=jnp.float32)
    h_ref[...] = h
    ms = jnp.mean(h * h, axis=-1, keepdims=True)
    n2 = (h * lax.rsqrt(ms + EPS) * g2_ref[...]).astype(jnp.bfloat16)
    n2_ref[...] = n2

    logits = jnp.dot(n2, wrt_ref[...], preferred_element_type=jnp.float32) + brt_ref[...]
    lane = lax.broadcasted_iota(jnp.int32, logits.shape, 1)
    big = jnp.int32(ROUTER_W)
    is_g = lane < N_GROUPS
    g_max = jnp.max(jnp.where(is_g, logits, -jnp.inf), axis=-1, keepdims=True)
    g_sel = jnp.min(jnp.where(is_g & (logits == g_max), lane, big), axis=-1, keepdims=True)
    g_w = 1.0 / jnp.sum(jnp.where(is_g, jnp.exp(logits - g_max), 0.0), axis=-1, keepdims=True)
    lo = N_GROUPS + g_sel * EXPERTS_PER_GROUP
    in_grp = (lane >= lo) & (lane < lo + EXPERTS_PER_GROUP)
    v1 = jnp.max(jnp.where(in_grp, logits, -jnp.inf), axis=-1, keepdims=True)
    i1 = jnp.min(jnp.where(in_grp & (logits == v1), lane, big), axis=-1, keepdims=True)
    rest = in_grp & (lane != i1)
    v2 = jnp.max(jnp.where(rest, logits, -jnp.inf), axis=-1, keepdims=True)
    i2 = jnp.min(jnp.where(rest & (logits == v2), lane, big), axis=-1, keepdims=True)
    e2 = jnp.exp(v2 - v1)
    p1 = g_w / (1.0 + e2)
    p2 = p1 * e2
    comb_ref[...] = jnp.where(lane == i1, p1, 0.0) + jnp.where(lane == i2, p2, 0.0)


def _merge(att, gret, proj, x2, wa, wr, wo, g2, w_router, b_router, tm):
    T = x2.shape[0]
    row = lambda i: (i, 0)
    const = lambda i: (0, 0)
    return pl.pallas_call(
        _merge_kernel,
        out_shape=(jax.ShapeDtypeStruct((T, D_MODEL), jnp.float32),
                   jax.ShapeDtypeStruct((T, D_MODEL), jnp.bfloat16),
                   jax.ShapeDtypeStruct((T, ROUTER_W), jnp.float32)),
        grid=(T // tm,),
        in_specs=[pl.BlockSpec((tm, 1024), row),
                  pl.BlockSpec((tm, 1024), row),
                  pl.BlockSpec((tm, 1024), lambda i: (i, C_GATT // 1024)),
                  pl.BlockSpec((tm, 1024), lambda i: (i, C_GRET // 1024)),
                  pl.BlockSpec((tm, 1024), row),
                  pl.BlockSpec((1024, 1024), const),
                  pl.BlockSpec((1024, 1024), const),
                  pl.BlockSpec((1024, 1024), const),
                  pl.BlockSpec((1, 1024), const),
                  pl.BlockSpec((1024, ROUTER_W), const),
                  pl.BlockSpec((1, ROUTER_W), const)],
        out_specs=(pl.BlockSpec((tm, 1024), row),
                   pl.BlockSpec((tm, 1024), row),
                   pl.BlockSpec((tm, ROUTER_W), row)),
        compiler_params=_cparams(("parallel",)),
        name="merge",
    )(att, gret, proj, proj, x2, wa, wr, wo, g2, w_router, b_router)


def _moe_kernel(n2_ref, comb_ref, h_ref, wgu_ref, wd_ref, o_ref, acc_ref):
    e = pl.program_id(1)

    @pl.when(e == 0)
    def _():
        acc_ref[...] = jnp.zeros(acc_ref.shape, jnp.float32)

    comb = comb_ref[...]
    lane = lax.broadcasted_iota(jnp.int32, comb.shape, 1)
    c = jnp.sum(jnp.where(lane == N_GROUPS + e, comb, 0.0), axis=-1, keepdims=True)
    gu = jnp.dot(n2_ref[...], wgu_ref[0], preferred_element_type=jnp.float32)
    hg = gu[:, :EXPERT_FF]
    hu = gu[:, EXPERT_FF:]
    act = (hg * jax.nn.sigmoid(hg) * hu * c).astype(jnp.bfloat16)
    acc_ref[...] += jnp.dot(act, wd_ref[0], preferred_element_type=jnp.float32)

    @pl.when(e == N_EXPERTS - 1)
    def _():
        o_ref[...] = h_ref[...] + acc_ref[...]


def _moe(n2, comb, h1, w_gu, w_d, tm):
    T = n2.shape[0]
    row = lambda i, e: (i, 0)
    return pl.pallas_call(
        _moe_kernel,
        out_shape=jax.ShapeDtypeStruct((T, D_MODEL), jnp.float32),
        grid=(T // tm, N_EXPERTS),
        in_specs=[pl.BlockSpec((tm, D_MODEL), row),
                  pl.BlockSpec((tm, ROUTER_W), row),
                  pl.BlockSpec((tm, D_MODEL), row),
                  pl.BlockSpec((1, D_MODEL, 2 * EXPERT_FF), lambda i, e: (e, 0, 0)),
                  pl.BlockSpec((1, EXPERT_FF, D_MODEL), lambda i, e: (e, 0, 0))],
        out_specs=pl.BlockSpec((tm, D_MODEL), row),
        scratch_shapes=[pltpu.VMEM((tm, D_MODEL), jnp.float32)],
        compiler_params=_cparams(("parallel", "arbitrary")),
        name="moe",
    )(n2, comb, h1, w_gu, w_d)


def _relayout_w_in(w):
    pad = jnp.zeros((w.shape[0], C_RQ - (C_WIDX + IDX_HEADS)), w.dtype)
    return jnp.concatenate(
        [w[:, 0:1024],
         w[:, 1280:1792],
         w[:, 1024:1280],
         w[:, 1792:1864],
         pad,
         w[:, 1864:]], axis=1)


def _layer(h, norm1_g, w_in, kv_norm_g, w_kv_up, q_norm_g, k_norm_g, idx_k_norm_g,
           ret_norm_g, w_att_branch, w_ret_branch, w_out, norm2_g, w_group_router,
           b_group_router, w_expert_router, b_expert_router, w_exp_gate, w_exp_up, w_exp_down):
    B, S, D = h.shape
    assert D == D_MODEL and S % KEY_CHUNK == 0 and S % Q_BLOCK == 0
    T = B * S
    n_sel = min(TOPK_MAX, S // 4)
    tm = 512 if T % 512 == 0 else 256
    bf = jnp.bfloat16
    f32 = jnp.float32

    x2 = h.reshape(T, D)
    w_p = _relayout_w_in(w_in).astype(bf)
    proj = _proj(x2, norm1_g.reshape(1, D).astype(f32), w_p, tm, 1024)

    idx_g_pad = jnp.concatenate([idx_k_norm_g.astype(f32), jnp.zeros((LANES - IDX_DIM,), f32)])
    qn, k_sh, v_sh, klo, khi, widx = _prep(
        proj, q_norm_g.reshape(1, -1).astype(f32), kv_norm_g.reshape(1, -1).astype(f32),
        w_kv_up.astype(bf), k_norm_g.reshape(1, -1).astype(f32), idx_g_pad.reshape(1, LANES), tm)

    att = _attn(qn, proj, widx, klo, khi, k_sh, v_sh, B, S, n_sel)
    gret = _ret(proj, ret_norm_g.reshape(1, -1).astype(f32), B, S)

    w_router = jnp.concatenate(
        [w_group_router, w_expert_router,
         jnp.zeros((D, ROUTER_W - N_GROUPS - N_EXPERTS), w_group_router.dtype)], axis=1).astype(bf)
    b_router = jnp.concatenate(
        [b_group_router, b_expert_router,
         jnp.zeros((ROUTER_W - N_GROUPS - N_EXPERTS,), b_group_router.dtype)]).reshape(1, ROUTER_W)
    h1, n2, comb = _merge(att, gret, proj, x2, w_att_branch.astype(bf), w_ret_branch.astype(bf),
                          w_out.astype(bf), norm2_g.reshape(1, D).astype(f32),
                          w_router, b_router.astype(f32), tm)

    w_gu = jnp.concatenate([w_exp_gate, w_exp_up], axis=-1).astype(bf)
    out = _moe(n2, comb, h1, w_gu, w_exp_down.astype(bf), tm)
    return out.reshape(B, S, D)


def kernel(x, norm1_g, w_in, kv_norm_g, w_kv_up, q_norm_g, k_norm_g, idx_k_norm_g, ret_norm_g,
           w_att_branch, w_ret_branch, w_out, norm2_g, w_group_router, b_group_router,
           w_expert_router, b_expert_router, w_exp_gate, w_exp_up, w_exp_down):
    h = x
    for l in range(norm1_g.shape[0]):
        h = _layer(h, norm1_g[l], w_in[l], kv_norm_g[l], w_kv_up[l], q_norm_g[l], k_norm_g[l],
                   idx_k_norm_g[l], ret_norm_g[l], w_att_branch[l], w_ret_branch[l], w_out[l],
                   norm2_g[l], w_group_router[l], b_group_router[l], w_expert_router[l],
                   b_expert_router[l], w_exp_gate[l], w_exp_up[l], w_exp_down[l])
    return h
```

```python
import functools
import math

import jax
import jax.numpy as jnp
from jax import lax
from jax.experimental import pallas as pl
from jax.experimental.pallas import tpu as pltpu

D_MODEL = 1024
N_ATT_HEADS = 8
ATT_HEAD_DIM = 128
KV_LORA = 256
IDX_HEADS = 8
IDX_DIM = 64
TOPK_MAX = 256
Q_BLOCK = 128
N_RET_HEADS = 4
RET_QK_DIM = 128
RET_V_DIM = 256
RET_CHUNK = 128
N_GROUPS = 4
EXPERTS_PER_GROUP = 4
N_EXPERTS = N_GROUPS * EXPERTS_PER_GROUP
EXPERT_FF = 256
EPS = 1e-6

LANES = 128
KEY_CHUNK = 256
QK_AUG = 256
VT_ROWS = 144
PROJ_W = 7168
C_QATT, C_QIDX, C_CKV, C_KIDX, C_WIDX = 0, 1024, 1536, 1792, 1856
C_RQ, C_RK, C_RV, C_RGATE, C_GATT, C_GRET = 2048, 2560, 3072, 4096, 5120, 6144
ROUTER_W = 128
NEG_BIG = -1e30
VMEM_LIMIT = 56 * 1024 * 1024

_INT_MIN = -(2 ** 31)


def _cparams(sem):
    return pltpu.CompilerParams(dimension_semantics=sem, vmem_limit_bytes=VMEM_LIMIT)


def _proj_kernel(x_ref, g_ref, w_ref, o_ref, n_ref):
    @pl.when(pl.program_id(1) == 0)
    def _():
        x = x_ref[...]
        ms = jnp.mean(x * x, axis=-1, keepdims=True)
        n_ref[...] = (x * lax.rsqrt(ms + EPS) * g_ref[...]).astype(jnp.bfloat16)

    o_ref[...] = jnp.dot(n_ref[...], w_ref[...],
                         preferred_element_type=jnp.float32).astype(o_ref.dtype)


def _proj(x2, g1, w_p, tm, tn):
    T = x2.shape[0]
    return pl.pallas_call(
        _proj_kernel,
        out_shape=jax.ShapeDtypeStruct((T, PROJ_W), jnp.bfloat16),
        grid=(T // tm, PROJ_W // tn),
        in_specs=[pl.BlockSpec((tm, D_MODEL), lambda i, j: (i, 0)),
                  pl.BlockSpec((1, D_MODEL), lambda i, j: (0, 0)),
                  pl.BlockSpec((D_MODEL, tn), lambda i, j: (0, j))],
        out_specs=pl.BlockSpec((tm, tn), lambda i, j: (i, j)),
        scratch_shapes=[pltpu.VMEM((tm, D_MODEL), jnp.bfloat16)],
        compiler_params=_cparams(("parallel", "arbitrary")),
        name="proj",
    )(x2, g1, w_p)


def _prep_kernel(S, q_ref, c_ref, qg_ref, kvg_ref, wkv_ref, kg_ref, ig_ref,
                 qa_ref, ka_ref, vt_ref, kd_ref, w_ref):
    tm = q_ref.shape[0]
    lane = lax.broadcasted_iota(jnp.int32, (tm, LANES), 1)

    qg = qg_ref[...]
    for h in range(N_ATT_HEADS):
        qh = q_ref[:, h * ATT_HEAD_DIM:(h + 1) * ATT_HEAD_DIM].astype(jnp.float32)
        ms = jnp.mean(qh * qh, axis=-1, keepdims=True)
        qa_ref[:, h * QK_AUG:h * QK_AUG + ATT_HEAD_DIM] = (
            qh * lax.rsqrt(ms + EPS) * qg * (ATT_HEAD_DIM ** -0.5)).astype(jnp.bfloat16)
        slope = 2.0 ** (-8.0 * (h + 1) / N_ATT_HEADS)
        qa_ref[:, h * QK_AUG + ATT_HEAD_DIM:(h + 1) * QK_AUG] = jnp.where(
            lane < 2, slope, 0.0).astype(jnp.bfloat16)

    c = c_ref[:, 0:KV_LORA].astype(jnp.float32)
    ms = jnp.mean(c * c, axis=-1, keepdims=True)
    cn = (c * lax.rsqrt(ms + EPS) * kvg_ref[...]).astype(jnp.bfloat16)
    kv = jnp.dot(cn, wkv_ref[...], preferred_element_type=jnp.float32)
    k = kv[:, :ATT_HEAD_DIM]
    ms = jnp.mean(k * k, axis=-1, keepdims=True)
    ka_ref[:, 0:ATT_HEAD_DIM] = (k * lax.rsqrt(ms + EPS) * kg_ref[...]).astype(jnp.bfloat16)
    base = (pl.program_id(0) % (S // tm)) * tm
    pos = base + lax.broadcasted_iota(jnp.int32, (tm, LANES), 0)
    pos_lo = pos & 255
    ka_ref[:, ATT_HEAD_DIM:] = jnp.where(
        lane == 0, pos - pos_lo, jnp.where(lane == 1, pos_lo, 0)).astype(jnp.float32).astype(jnp.bfloat16)
    v = kv[:, ATT_HEAD_DIM:]
    ones_row = jnp.where(lax.broadcasted_iota(jnp.int32, (VT_ROWS - ATT_HEAD_DIM, KEY_CHUNK), 0) == 0,
                         1.0, 0.0).astype(jnp.bfloat16)
    for u in range(tm // KEY_CHUNK):
        vt_ref[u, 0:ATT_HEAD_DIM, :] = v[u * KEY_CHUNK:(u + 1) * KEY_CHUNK, :].T.astype(jnp.bfloat16)
        vt_ref[u, ATT_HEAD_DIM:, :] = ones_row

    blk = c_ref[:, C_KIDX - C_CKV:C_KIDX - C_CKV + LANES].astype(jnp.float32)
    is_k = lane < IDX_DIM
    ms = jnp.sum(jnp.where(is_k, blk * blk, 0.0), axis=-1, keepdims=True) * (1.0 / IDX_DIM)
    kn = jnp.where(is_k, blk * lax.rsqrt(ms + EPS) * ig_ref[...], 0.0)
    kd_ref[...] = (kn + pltpu.roll(kn, IDX_DIM, 1)).astype(jnp.bfloat16)
    w_ref[...] = blk * ((IDX_HEADS ** -0.5) * (IDX_DIM ** -0.5))


def _prep(proj, q_norm_g, kv_norm_g, w_kv_up, k_norm_g, idx_g_pad, tm, S):
    T = proj.shape[0]
    row = lambda i: (i, 0)
    const = lambda i: (0, 0)
    return pl.pallas_call(
        functools.partial(_prep_kernel, S),
        out_shape=(jax.ShapeDtypeStruct((T, N_ATT_HEADS * QK_AUG), jnp.bfloat16),
                   jax.ShapeDtypeStruct((T, QK_AUG), jnp.bfloat16),
                   jax.ShapeDtypeStruct((T // KEY_CHUNK, VT_ROWS, KEY_CHUNK), jnp.bfloat16),
                   jax.ShapeDtypeStruct((T, LANES), jnp.bfloat16),
                   jax.ShapeDtypeStruct((T, LANES), jnp.float32)),
        grid=(T // tm,),
        in_specs=[pl.BlockSpec((tm, 1024), lambda i: (i, C_QATT // 1024)),
                  pl.BlockSpec((tm, 512), lambda i: (i, C_CKV // 512)),
                  pl.BlockSpec((1, ATT_HEAD_DIM), const),
                  pl.BlockSpec((1, KV_LORA), const),
                  pl.BlockSpec((KV_LORA, 2 * ATT_HEAD_DIM), const),
                  pl.BlockSpec((1, ATT_HEAD_DIM), const),
                  pl.BlockSpec((1, LANES), const)],
        out_specs=(pl.BlockSpec((tm, N_ATT_HEADS * QK_AUG), row),
                   pl.BlockSpec((tm, QK_AUG), row),
                   pl.BlockSpec((tm // KEY_CHUNK, VT_ROWS, KEY_CHUNK), lambda i: (i, 0, 0)),
                   pl.BlockSpec((tm, LANES), row),
                   pl.BlockSpec((tm, LANES), row)),
        compiler_params=_cparams(("parallel",)),
        name="prep",
    )(proj, proj, q_norm_g, kv_norm_g, w_kv_up, k_norm_g, idx_g_pad)


def _attn_kernel(n_sel, qa_ref, qi_ref, w_ref, kd_ref, ka_ref, vt_ref, o_ref,
                 sc_ref, qs_ref, m_ref, acc_ref):
    j = pl.program_id(1)
    Q, KC = Q_BLOCK, KEY_CHUNK
    nch = (j * Q + Q + KC - 1) // KC
    q_pos = j * Q + lax.broadcasted_iota(jnp.int32, (1, Q), 1)
    row_in_chunk = lax.broadcasted_iota(jnp.int32, (KC, Q), 0)
    nt = (((1,), (1,)), ((), ()))

    lane = lax.broadcasted_iota(jnp.int32, (Q, LANES), 1)
    for p in range(IDX_HEADS // 2):
        qp = qi_ref[:, p * LANES:(p + 1) * LANES].astype(jnp.float32)
        qs_ref[p, 0:Q, :] = jnp.where(lane < IDX_DIM, qp, 0.0).astype(jnp.bfloat16)
        qs_ref[p, Q:2 * Q, :] = jnp.where(lane >= IDX_DIM, qp, 0.0).astype(jnp.bfloat16)
    w_t = w_ref[...].T

    def score_chunk(c, carry):
        r0 = pl.multiple_of(c * KC, KC)
        kd = kd_ref[pl.ds(r0, KC), :]
        s = jnp.zeros((KC, Q), jnp.float32)
        for p in range(IDX_HEADS // 2):
            d = lax.dot_general(kd, qs_ref[p], nt, preferred_element_type=jnp.float32)
            h0 = IDX_DIM + 2 * p
            s = (s + w_t[h0:h0 + 1, :] * jnp.maximum(d[:, :Q], 0.0)
                 + w_t[h0 + 1:h0 + 2, :] * jnp.maximum(d[:, Q:], 0.0))
        sc_ref[pl.ds(r0, KC), :] = jnp.where(r0 + row_in_chunk <= q_pos, s, -jnp.inf)
        return carry

    lax.fori_loop(0, nch, score_chunk, 0)

    k_row = jnp.minimum(q_pos + 1, n_sel).astype(jnp.float32)

    def count_rows(pred):
        def body(c, acc):
            r0 = pl.multiple_of(c * KC, KC)
            hit = jnp.where(pred(sc_ref[pl.ds(r0, KC), :]), 1.0, 0.0)
            parts = [hit[u * 8:(u + 1) * 8, :] for u in range(KC // 8)]
            while len(parts) > 1:
                parts = [a + b for a, b in zip(parts[0::2], parts[1::2])]
            return acc + parts[0]

        acc = lax.fori_loop(0, nch, body, jnp.zeros((8, Q), jnp.float32))
        return jnp.sum(acc, axis=0, keepdims=True)

    def key_to_float(u):
        key = u ^ _INT_MIN
        bits = jnp.where(key >= 0, key, key ^ 0x7FFFFFFF)
        return lax.bitcast_convert_type(bits, jnp.float32)

    def bit_body(i, carry):
        prefix, cnt_best = carry
        cand_u = prefix | lax.shift_left(jnp.int32(1), 31 - i)
        cand = key_to_float(cand_u)
        cnt = count_rows(lambda blk: blk >= cand)
        take = cnt >= k_row
        return jnp.where(take, cand_u, prefix), jnp.where(take, cnt, cnt_best)

    prefix, cnt_best = lax.fori_loop(
        0, 32, bit_body, (jnp.zeros((1, Q), jnp.int32), k_row))
    thr = key_to_float(prefix)

    @pl.when(jnp.max(cnt_best - k_row) > 0.0)
    def _():
        need = k_row - count_rows(lambda blk: blk > thr)
        r_i = lax.broadcasted_iota(jnp.int32, (KC, KC), 0)
        c_i = lax.broadcasted_iota(jnp.int32, (KC, KC), 1)
        tri = jnp.where(c_i <= r_i, 1.0, 0.0).astype(jnp.bfloat16)

        def drop_body(c, seen):
            r0 = pl.multiple_of(c * KC, KC)
            blk = sc_ref[pl.ds(r0, KC), :]
            eq = blk == thr
            eq_f = jnp.where(eq, 1.0, 0.0)
            incl = jnp.dot(tri, eq_f.astype(jnp.bfloat16), preferred_element_type=jnp.float32)
            rank = seen + incl - eq_f
            sc_ref[pl.ds(r0, KC), :] = jnp.where(eq & (rank >= need), -jnp.inf, blk)
            return seen + jnp.sum(eq_f, axis=0, keepdims=True)

        lax.fori_loop(0, nch, drop_body, jnp.zeros((1, Q), jnp.float32))

    m_ref[...] = jnp.full(m_ref.shape, NEG_BIG, jnp.float32)
    acc_ref[...] = jnp.zeros(acc_ref.shape, jnp.float32)

    def att_chunk(c, carry):
        r0 = pl.multiple_of(c * KC, KC)
        ka = ka_ref[pl.ds(r0, KC), :]
        vt = vt_ref[c]
        mask_bias = jnp.where(sc_ref[pl.ds(r0, KC), :] >= thr, 0.0, NEG_BIG)
        for p in range(N_ATT_HEADS // 2):
            q_pair = jnp.concatenate(
                [qa_ref[:, (2 * p) * QK_AUG:(2 * p + 1) * QK_AUG],
                 qa_ref[:, (2 * p + 1) * QK_AUG:(2 * p + 2) * QK_AUG]], axis=0)
            lg = lax.dot_general(ka, q_pair, nt, preferred_element_type=jnp.float32)
            ps, alphas = [], []
            for hh in range(2):
                h = 2 * p + hh
                lgh = lg[:, hh * Q:(hh + 1) * Q] + mask_bias
                m_old = m_ref[h][0:1, :]
                m_new = jnp.maximum(m_old, jnp.max(lgh, axis=0, keepdims=True))
                alphas.append(jnp.exp(m_old - m_new))
                ps.append(jnp.exp(lgh - m_new).astype(jnp.bfloat16))
                m_ref[h] = jnp.broadcast_to(m_new, (8, Q))
            pv = jnp.dot(vt, jnp.concatenate(ps, axis=1), preferred_element_type=jnp.float32)
            acc_ref[p] = acc_ref[p] * jnp.concatenate(alphas, axis=1) + pv
        return carry

    lax.fori_loop(0, nch, att_chunk, 0)

    for h in range(N_ATT_HEADS):
        a = acc_ref[h // 2][:, (h % 2) * Q:(h % 2 + 1) * Q]
        out_t = a[0:ATT_HEAD_DIM, :] / a[ATT_HEAD_DIM:ATT_HEAD_DIM + 1, :]
        o_ref[:, h * ATT_HEAD_DIM:(h + 1) * ATT_HEAD_DIM] = out_t.T.astype(o_ref.dtype)


def _attn(qa, proj, widx, kd, ka, vt, B, S, n_sel):
    T = B * S
    nb = S // Q_BLOCK
    n_kc = S // KEY_CHUNK
    qrow = lambda b, j: (b * nb + j, 0)
    per_b = lambda b, j: (b, 0)
    return pl.pallas_call(
        functools.partial(_attn_kernel, n_sel),
        out_shape=jax.ShapeDtypeStruct((T, D_MODEL), jnp.bfloat16),
        grid=(B, nb),
        in_specs=[pl.BlockSpec((Q_BLOCK, N_ATT_HEADS * QK_AUG), qrow),
                  pl.BlockSpec((Q_BLOCK, 512), lambda b, j: (b * nb + j, C_QIDX // 512)),
                  pl.BlockSpec((Q_BLOCK, LANES), qrow),
                  pl.BlockSpec((S, LANES), per_b),
                  pl.BlockSpec((S, QK_AUG), per_b),
                  pl.BlockSpec((n_kc, VT_ROWS, KEY_CHUNK), lambda b, j: (b, 0, 0))],
        out_specs=pl.BlockSpec((Q_BLOCK, D_MODEL), qrow),
        scratch_shapes=[pltpu.VMEM((S, Q_BLOCK), jnp.float32),
                        pltpu.VMEM((IDX_HEADS // 2, 2 * Q_BLOCK, LANES), jnp.bfloat16),
                        pltpu.VMEM((N_ATT_HEADS, 8, Q_BLOCK), jnp.float32),
                        pltpu.VMEM((N_ATT_HEADS // 2, VT_ROWS, 2 * Q_BLOCK), jnp.float32)],
        compiler_params=_cparams(("parallel", "arbitrary")),
        name="attn",
    )(qa, proj, widx, kd, ka, vt)


def _ret_kernel(rq_ref, rk_ref, rv_ref, gate_ref, g_ref, o_ref, state_ref):
    C = RET_CHUNK

    @pl.when(pl.program_id(1) == 0)
    def _():
        state_ref[...] = jnp.zeros(state_ref.shape, jnp.float32)

    ii = lax.broadcasted_iota(jnp.int32, (C, C), 0)
    jj = lax.broadcasted_iota(jnp.int32, (C, C), 1)
    rel = (ii - jj).astype(jnp.float32)
    pos = lax.broadcasted_iota(jnp.int32, (C, 1), 0).astype(jnp.float32)
    for h in range(N_RET_HEADS):
        log_g = math.log(1.0 - 2.0 ** (-5.0 - h))
        decay_intra = jnp.where(rel >= 0, jnp.exp(rel * log_g), 0.0)
        q_decay = jnp.exp((pos + 1.0) * log_g)
        k_decay = jnp.exp((C - 1.0 - pos) * log_g)
        chunk_decay = math.exp(C * log_g)

        q = rq_ref[:, h * RET_QK_DIM:(h + 1) * RET_QK_DIM].astype(jnp.float32)
        k = rk_ref[:, h * RET_QK_DIM:(h + 1) * RET_QK_DIM].astype(jnp.float32) * (RET_QK_DIM ** -0.5)
        v = rv_ref[:, h * RET_V_DIM:(h + 1) * RET_V_DIM]
        state = state_ref[h]

        qk = lax.dot_general(q.astype(jnp.bfloat16), k.astype(jnp.bfloat16),
                             (((1,), (1,)), ((), ())), preferred_element_type=jnp.float32)
        intra = (qk * decay_intra).astype(jnp.bfloat16)
        o = (jnp.dot(intra, v, preferred_element_type=jnp.float32)
             + jnp.dot((q * q_decay).astype(jnp.bfloat16), state.astype(jnp.bfloat16),
                       preferred_element_type=jnp.float32))
        kd_t = (k * k_decay).T.astype(jnp.bfloat16)
        state_ref[h] = chunk_decay * state + jnp.dot(kd_t, v, preferred_element_type=jnp.float32)

        mu = jnp.mean(o, axis=-1, keepdims=True)
        var = jnp.mean(jnp.square(o - mu), axis=-1, keepdims=True)
        sl = slice(h * RET_V_DIM, (h + 1) * RET_V_DIM)
        y = (o - mu) * lax.rsqrt(var + EPS) * g_ref[:, sl]
        gate = gate_ref[:, sl].astype(jnp.float32)
        o_ref[:, sl] = (gate * jax.nn.sigmoid(gate) * y).astype(o_ref.dtype)


def _ret(proj, ret_g, B, S):
    T = B * S
    n = S // RET_CHUNK
    C = RET_CHUNK
    return pl.pallas_call(
        _ret_kernel,
        out_shape=jax.ShapeDtypeStruct((T, N_RET_HEADS * RET_V_DIM), jnp.bfloat16),
        grid=(B, n),
        in_specs=[pl.BlockSpec((C, 512), lambda b, i: (b * n + i, C_RQ // 512)),
                  pl.BlockSpec((C, 512), lambda b, i: (b * n + i, C_RK // 512)),
                  pl.BlockSpec((C, 1024), lambda b, i: (b * n + i, C_RV // 1024)),
                  pl.BlockSpec((C, 1024), lambda b, i: (b * n + i, C_RGATE // 1024)),
                  pl.BlockSpec((1, 1024), lambda b, i: (0, 0))],
        out_specs=pl.BlockSpec((C, 1024), lambda b, i: (b * n + i, 0)),
        scratch_shapes=[pltpu.VMEM((N_RET_HEADS, RET_QK_DIM, RET_V_DIM), jnp.float32)],
        compiler_params=_cparams(("parallel", "arbitrary")),
        name="ret",
    )(proj, proj, proj, proj, ret_g)


def _merge_kernel(att_ref, ret_ref, ga_ref, gr_ref, x_ref, wa_ref, wr_ref, wo_ref,
                  g2_ref, wrt_ref, brt_ref, h_ref, n2_ref, comb_ref):
    y_att = jnp.dot(att_ref[...], wa_ref[...], preferred_element_type=jnp.float32)
    y_ret = jnp.dot(ret_ref[...], wr_ref[...], preferred_element_type=jnp.float32)
    mixed = (jax.nn.sigmoid(ga_ref[...].astype(jnp.float32)) * y_att
             + jax.nn.sigmoid(gr_ref[...].astype(jnp.float32)) * y_ret)
    h = x_ref[...] + jnp.dot(mixed.astype(jnp.bfloat16), wo_ref[...],
                             preferred_element_type=jnp.float32)
    h_ref[...] = h
    ms = jnp.mean(h * h, axis=-1, keepdims=True)
    n2 = (h * lax.rsqrt(ms + EPS) * g2_ref[...]).astype(jnp.bfloat16)
    n2_ref[...] = n2

    logits = jnp.dot(n2, wrt_ref[...], preferred_element_type=jnp.float32) + brt_ref[...]
    lane = lax.broadcasted_iota(jnp.int32, logits.shape, 1)
    big = jnp.int32(ROUTER_W)
    is_g = lane < N_GROUPS
    g_max = jnp.max(jnp.where(is_g, logits, -jnp.inf), axis=-1, keepdims=True)
    g_sel = jnp.min(jnp.where(is_g & (logits == g_max), lane, big), axis=-1, keepdims=True)
    g_w = 1.0 / jnp.sum(jnp.where(is_g, jnp.exp(logits - g_max), 0.0), axis=-1, keepdims=True)
    lo = N_GROUPS + g_sel * EXPERTS_PER_GROUP
    in_grp = (lane >= lo) & (lane < lo + EXPERTS_PER_GROUP)
    v1 = jnp.max(jnp.where(in_grp, logits, -jnp.inf), axis=-1, keepdims=True)
    i1 = jnp.min(jnp.where(in_grp & (logits == v1), lane, big), axis=-1, keepdims=True)
    rest = in_grp & (lane != i1)
    v2 = jnp.max(jnp.where(rest, logits, -jnp.inf), axis=-1, keepdims=True)
    i2 = jnp.min(jnp.where(rest & (logits == v2), lane, big), axis=-1, keepdims=True)
    e2 = jnp.exp(v2 - v1)
    p1 = g_w / (1.0 + e2)
    p2 = p1 * e2
    comb_ref[...] = jnp.where(lane == i1, p1, 0.0) + jnp.where(lane == i2, p2, 0.0)


def _merge(att, gret, proj, x2, wa, wr, wo, g2, w_router, b_router, tm):
    T = x2.shape[0]
    row = lambda i: (i, 0)
    const = lambda i: (0, 0)
    return pl.pallas_call(
        _merge_kernel,
        out_shape=(jax.ShapeDtypeStruct((T, D_MODEL), jnp.float32),
                   jax.ShapeDtypeStruct((T, D_MODEL), jnp.bfloat16),
                   jax.ShapeDtypeStruct((T, ROUTER_W), jnp.float32)),
        grid=(T // tm,),
        in_specs=[pl.BlockSpec((tm, 1024), row),
                  pl.BlockSpec((tm, 1024), row),
                  pl.BlockSpec((tm, 1024), lambda i: (i, C_GATT // 1024)),
                  pl.BlockSpec((tm, 1024), lambda i: (i, C_GRET // 1024)),
                  pl.BlockSpec((tm, 1024), row),
                  pl.BlockSpec((1024, 1024), const),
                  pl.BlockSpec((1024, 1024), const),
                  pl.BlockSpec((1024, 1024), const),
                  pl.BlockSpec((1, 1024), const),
                  pl.BlockSpec((1024, ROUTER_W), const),
                  pl.BlockSpec((1, ROUTER_W), const)],
        out_specs=(pl.BlockSpec((tm, 1024), row),
                   pl.BlockSpec((tm, 1024), row),
                   pl.BlockSpec((tm, ROUTER_W), row)),
        compiler_params=_cparams(("parallel",)),
        name="merge",
    )(att, gret, proj, proj, x2, wa, wr, wo, g2, w_router, b_router)


def _moe_kernel(n2_ref, comb_ref, h_ref, wgu_ref, wd_ref, o_ref, acc_ref):
    e = pl.program_id(1)

    @pl.when(e == 0)
    def _():
        acc_ref[...] = jnp.zeros(acc_ref.shape, jnp.float32)

    comb = comb_ref[...]
    lane = lax.broadcasted_iota(jnp.int32, comb.shape, 1)
    c = jnp.sum(jnp.where(lane == N_GROUPS + e, comb, 0.0), axis=-1, keepdims=True)
    gu = jnp.dot(n2_ref[...], wgu_ref[0], preferred_element_type=jnp.float32)
    hg = gu[:, :EXPERT_FF]
    hu = gu[:, EXPERT_FF:]
    act = (hg * jax.nn.sigmoid(hg) * hu * c).astype(jnp.bfloat16)
    acc_ref[...] += jnp.dot(act, wd_ref[0], preferred_element_type=jnp.float32)

    @pl.when(e == N_EXPERTS - 1)
    def _():
        o_ref[...] = h_ref[...] + acc_ref[...]


def _moe(n2, comb, h1, w_gu, w_d, tm):
    T = n2.shape[0]
    row = lambda i, e: (i, 0)
    return pl.pallas_call(
        _moe_kernel,
        out_shape=jax.ShapeDtypeStruct((T, D_MODEL), jnp.float32),
        grid=(T // tm, N_EXPERTS),
        in_specs=[pl.BlockSpec((tm, D_MODEL), row),
                  pl.BlockSpec((tm, ROUTER_W), row),
                  pl.BlockSpec((tm, D_MODEL), row),
                  pl.BlockSpec((1, D_MODEL, 2 * EXPERT_FF), lambda i, e: (e, 0, 0)),
                  pl.BlockSpec((1, EXPERT_FF, D_MODEL), lambda i, e: (e, 0, 0))],
        out_specs=pl.BlockSpec((tm, D_MODEL), row),
        scratch_shapes=[pltpu.VMEM((tm, D_MODEL), jnp.float32)],
        compiler_params=_cparams(("parallel", "arbitrary")),
        name="moe",
    )(n2, comb, h1, w_gu, w_d)


def _relayout_w_in(w):
    pad = jnp.zeros((w.shape[0], C_RQ - (C_WIDX + IDX_HEADS)), w.dtype)
    return jnp.concatenate(
        [w[:, 0:1024],
         w[:, 1280:1792],
         w[:, 1024:1280],
         w[:, 1792:1864],
         pad,
         w[:, 1864:]], axis=1)


def _layer(h, norm1_g, w_in, kv_norm_g, w_kv_up, q_norm_g, k_norm_g, idx_k_norm_g,
           ret_norm_g, w_att_branch, w_ret_branch, w_out, norm2_g, w_group_router,
           b_group_router, w_expert_router, b_expert_router, w_exp_gate, w_exp_up, w_exp_down):
    B, S, D = h.shape
    assert D == D_MODEL and S % KEY_CHUNK == 0 and S % Q_BLOCK == 0 and S <= 65536
    T = B * S
    n_sel = min(TOPK_MAX, S // 4)
    tm = 512 if S % 512 == 0 else 256
    bf = jnp.bfloat16
    f32 = jnp.float32

    x2 = h.reshape(T, D)
    w_p = _relayout_w_in(w_in).astype(bf)
    proj = _proj(x2, norm1_g.reshape(1, D).astype(f32), w_p, tm, 1024)

    idx_g_pad = jnp.concatenate([idx_k_norm_g.astype(f32), jnp.zeros((LANES - IDX_DIM,), f32)])
    qa, ka, vt, kd, widx = _prep(
        proj, q_norm_g.reshape(1, -1).astype(f32), kv_norm_g.reshape(1, -1).astype(f32),
        w_kv_up.astype(bf), k_norm_g.reshape(1, -1).astype(f32), idx_g_pad.reshape(1, LANES), tm, S)

    att = _attn(qa, proj, widx, kd, ka, vt, B, S, n_sel)
    gret = _ret(proj, ret_norm_g.reshape(1, -1).astype(f32), B, S)

    w_router = jnp.concatenate(
        [w_group_router, w_expert_router,
         jnp.zeros((D, ROUTER_W - N_GROUPS - N_EXPERTS), w_group_router.dtype)], axis=1).astype(bf)
    b_router = jnp.concatenate(
        [b_group_router, b_expert_router,
         jnp.zeros((ROUTER_W - N_GROUPS - N_EXPERTS,), b_group_router.dtype)]).reshape(1, ROUTER_W)
    h1, n2, comb = _merge(att, gret, proj, x2, w_att_branch.astype(bf), w_ret_branch.astype(bf),
                          w_out.astype(bf), norm2_g.reshape(1, D).astype(f32),
                          w_router, b_router.astype(f32), tm)

    w_gu = jnp.concatenate([w_exp_gate, w_exp_up], axis=-1).astype(bf)
    out = _moe(n2, comb, h1, w_gu, w_exp_down.astype(bf), tm)
    return out.reshape(B, S, D)


def kernel(x, norm1_g, w_in, kv_norm_g, w_kv_up, q_norm_g, k_norm_g, idx_k_norm_g, ret_norm_g,
           w_att_branch, w_ret_branch, w_out, norm2_g, w_group_router, b_group_router,
           w_expert_router, b_expert_router, w_exp_gate, w_exp_up, w_exp_down):
    h = x
    for l in range(norm1_g.shape[0]):
        h = _layer(h, norm1_g[l], w_in[l], kv_norm_g[l], w_kv_up[l], q_norm_g[l], k_norm_g[l],
                   idx_k_norm_g[l], ret_norm_g[l], w_att_branch[l], w_ret_branch[l], w_out[l],
                   norm2_g[l], w_group_router[l], b_group_router[l], w_expert_router[l],
                   b_expert_router[l], w_exp_gate[l], w_exp_up[l], w_exp_down[l])
    return h
```

```python
import functools
import math

import jax
import jax.numpy as jnp
from jax import lax
from jax.experimental import pallas as pl
from jax.experimental.pallas import tpu as pltpu

D_MODEL = 1024
N_ATT_HEADS = 8
ATT_HEAD_DIM = 128
KV_LORA = 256
IDX_HEADS = 8
IDX_DIM = 64
TOPK_MAX = 256
Q_BLOCK = 128
N_RET_HEADS = 4
RET_QK_DIM = 128
RET_V_DIM = 256
RET_CHUNK = 128
N_GROUPS = 4
EXPERTS_PER_GROUP = 4
N_EXPERTS = N_GROUPS * EXPERTS_PER_GROUP
EXPERT_FF = 256
EPS = 1e-6

LANES = 128
KEY_CHUNK = 256
QK_AUG = 256
VT_ROWS = 144
PROJ_W = 7168
C_QATT, C_QIDX, C_CKV, C_KIDX, C_WIDX = 0, 1024, 1536, 1792, 1856
C_RQ, C_RK, C_RV, C_RGATE, C_GATT, C_GRET = 2048, 2560, 3072, 4096, 5120, 6144
ROUTER_W = 128
NEG_BIG = -1e30
VMEM_LIMIT = 56 * 1024 * 1024

_INT_MIN = -(2 ** 31)


def _cparams(sem):
    return pltpu.CompilerParams(dimension_semantics=sem, vmem_limit_bytes=VMEM_LIMIT)


def _proj_kernel(tn, x_ref, g_ref, w_ref, o_ref):
    x = x_ref[...]
    ms = jnp.mean(x * x, axis=-1, keepdims=True)
    n = (x * lax.rsqrt(ms + EPS) * g_ref[...]).astype(jnp.bfloat16)
    for c in range(PROJ_W // tn):
        o_ref[:, c * tn:(c + 1) * tn] = jnp.dot(
            n, w_ref[:, c * tn:(c + 1) * tn], preferred_element_type=jnp.float32).astype(o_ref.dtype)


def _proj(x2, g1, w_p, tm, tn):
    T = x2.shape[0]
    return pl.pallas_call(
        functools.partial(_proj_kernel, tn),
        out_shape=jax.ShapeDtypeStruct((T, PROJ_W), jnp.bfloat16),
        grid=(T // tm,),
        in_specs=[pl.BlockSpec((tm, D_MODEL), lambda i: (i, 0)),
                  pl.BlockSpec((1, D_MODEL), lambda i: (0, 0)),
                  pl.BlockSpec((D_MODEL, PROJ_W), lambda i: (0, 0))],
        out_specs=pl.BlockSpec((tm, PROJ_W), lambda i: (i, 0)),
        compiler_params=_cparams(("parallel",)),
        name="proj",
    )(x2, g1, w_p)


def _prep_kernel(S, q_ref, c_ref, qg_ref, kvg_ref, wkv_ref, kg_ref, ig_ref,
                 qa_ref, ka_ref, vt_ref, kd_ref, w_ref):
    tm = q_ref.shape[0]
    lane = lax.broadcasted_iota(jnp.int32, (tm, LANES), 1)

    qg = qg_ref[...]
    for h in range(N_ATT_HEADS):
        qh = q_ref[:, h * ATT_HEAD_DIM:(h + 1) * ATT_HEAD_DIM].astype(jnp.float32)
        ms = jnp.mean(qh * qh, axis=-1, keepdims=True)
        qa_ref[:, h * QK_AUG:h * QK_AUG + ATT_HEAD_DIM] = (
            qh * lax.rsqrt(ms + EPS) * qg * (ATT_HEAD_DIM ** -0.5)).astype(jnp.bfloat16)
        slope = 2.0 ** (-8.0 * (h + 1) / N_ATT_HEADS)
        qa_ref[:, h * QK_AUG + ATT_HEAD_DIM:(h + 1) * QK_AUG] = jnp.where(
            lane < 2, slope, 0.0).astype(jnp.bfloat16)

    c = c_ref[:, 0:KV_LORA].astype(jnp.float32)
    ms = jnp.mean(c * c, axis=-1, keepdims=True)
    cn = (c * lax.rsqrt(ms + EPS) * kvg_ref[...]).astype(jnp.bfloat16)
    kv = jnp.dot(cn, wkv_ref[...], preferred_element_type=jnp.float32)
    k = kv[:, :ATT_HEAD_DIM]
    ms = jnp.mean(k * k, axis=-1, keepdims=True)
    ka_ref[:, 0:ATT_HEAD_DIM] = (k * lax.rsqrt(ms + EPS) * kg_ref[...]).astype(jnp.bfloat16)
    base = (pl.program_id(0) % (S // tm)) * tm
    pos = base + lax.broadcasted_iota(jnp.int32, (tm, LANES), 0)
    pos_lo = pos & 255
    ka_ref[:, ATT_HEAD_DIM:] = jnp.where(
        lane == 0, pos - pos_lo, jnp.where(lane == 1, pos_lo, 0)).astype(jnp.float32).astype(jnp.bfloat16)
    v = kv[:, ATT_HEAD_DIM:]
    ones_row = jnp.where(lax.broadcasted_iota(jnp.int32, (VT_ROWS - ATT_HEAD_DIM, KEY_CHUNK), 0) == 0,
                         1.0, 0.0).astype(jnp.bfloat16)
    for u in range(tm // KEY_CHUNK):
        vt_ref[u, 0:ATT_HEAD_DIM, :] = v[u * KEY_CHUNK:(u + 1) * KEY_CHUNK, :].T.astype(jnp.bfloat16)
        vt_ref[u, ATT_HEAD_DIM:, :] = ones_row

    blk = c_ref[:, C_KIDX - C_CKV:C_KIDX - C_CKV + LANES].astype(jnp.float32)
    is_k = lane < IDX_DIM
    ms = jnp.sum(jnp.where(is_k, blk * blk, 0.0), axis=-1, keepdims=True) * (1.0 / IDX_DIM)
    kn = jnp.where(is_k, blk * lax.rsqrt(ms + EPS) * ig_ref[...], 0.0)
    kd_ref[...] = (kn + pltpu.roll(kn, IDX_DIM, 1)).astype(jnp.bfloat16)
    w_ref[...] = blk * ((IDX_HEADS ** -0.5) * (IDX_DIM ** -0.5))


def _prep(proj, q_norm_g, kv_norm_g, w_kv_up, k_norm_g, idx_g_pad, tm, S):
    T = proj.shape[0]
    row = lambda i: (i, 0)
    const = lambda i: (0, 0)
    return pl.pallas_call(
        functools.partial(_prep_kernel, S),
        out_shape=(jax.ShapeDtypeStruct((T, N_ATT_HEADS * QK_AUG), jnp.bfloat16),
                   jax.ShapeDtypeStruct((T, QK_AUG), jnp.bfloat16),
                   jax.ShapeDtypeStruct((T // KEY_CHUNK, VT_ROWS, KEY_CHUNK), jnp.bfloat16),
                   jax.ShapeDtypeStruct((T, LANES), jnp.bfloat16),
                   jax.ShapeDtypeStruct((T, LANES), jnp.float32)),
        grid=(T // tm,),
        in_specs=[pl.BlockSpec((tm, 1024), lambda i: (i, C_QATT // 1024)),
                  pl.BlockSpec((tm, 512), lambda i: (i, C_CKV // 512)),
                  pl.BlockSpec((1, ATT_HEAD_DIM), const),
                  pl.BlockSpec((1, KV_LORA), const),
                  pl.BlockSpec((KV_LORA, 2 * ATT_HEAD_DIM), const),
                  pl.BlockSpec((1, ATT_HEAD_DIM), const),
                  pl.BlockSpec((1, LANES), const)],
        out_specs=(pl.BlockSpec((tm, N_ATT_HEADS * QK_AUG), row),
                   pl.BlockSpec((tm, QK_AUG), row),
                   pl.BlockSpec((tm // KEY_CHUNK, VT_ROWS, KEY_CHUNK), lambda i: (i, 0, 0)),
                   pl.BlockSpec((tm, LANES), row),
                   pl.BlockSpec((tm, LANES), row)),
        compiler_params=_cparams(("parallel",)),
        name="prep",
    )(proj, proj, q_norm_g, kv_norm_g, w_kv_up, k_norm_g, idx_g_pad)


def _attn_kernel(n_sel, qa_ref, qi_ref, w_ref, kd_ref, ka_ref, vt_ref, o_ref,
                 sc_ref, qs_ref, m_ref, acc_ref):
    j = pl.program_id(1)
    Q, KC = Q_BLOCK, KEY_CHUNK
    nch = (j * Q + Q + KC - 1) // KC
    q_pos = j * Q + lax.broadcasted_iota(jnp.int32, (1, Q), 1)
    row_in_chunk = lax.broadcasted_iota(jnp.int32, (KC, Q), 0)
    nt = (((1,), (1,)), ((), ()))

    lane = lax.broadcasted_iota(jnp.int32, (Q, LANES), 1)
    for p in range(IDX_HEADS // 2):
        qp = qi_ref[:, p * LANES:(p + 1) * LANES].astype(jnp.float32)
        qs_ref[p, 0:Q, :] = jnp.where(lane < IDX_DIM, qp, 0.0).astype(jnp.bfloat16)
        qs_ref[p, Q:2 * Q, :] = jnp.where(lane >= IDX_DIM, qp, 0.0).astype(jnp.bfloat16)
    w_t = w_ref[...].T

    def score_chunk(c, carry):
        r0 = pl.multiple_of(c * KC, KC)
        kd = kd_ref[pl.ds(r0, KC), :]
        s = jnp.zeros((KC, Q), jnp.float32)
        for p in range(IDX_HEADS // 2):
            d = lax.dot_general(kd, qs_ref[p], nt, preferred_element_type=jnp.float32)
            h0 = IDX_DIM + 2 * p
            s = (s + w_t[h0:h0 + 1, :] * jnp.maximum(d[:, :Q], 0.0)
                 + w_t[h0 + 1:h0 + 2, :] * jnp.maximum(d[:, Q:], 0.0))
        sc_ref[pl.ds(r0, KC), :] = jnp.where(r0 + row_in_chunk <= q_pos, s, -jnp.inf)
        return carry

    lax.fori_loop(0, nch, score_chunk, 0)

    k_row = jnp.minimum(q_pos + 1, n_sel).astype(jnp.float32)

    def count_rows(pred):
        def body(c, acc):
            r0 = pl.multiple_of(c * KC, KC)
            hit = jnp.where(pred(sc_ref[pl.ds(r0, KC), :]), 1.0, 0.0)
            parts = [hit[u * 8:(u + 1) * 8, :] for u in range(KC // 8)]
            while len(parts) > 1:
                parts = [a + b for a, b in zip(parts[0::2], parts[1::2])]
            return acc + parts[0]

        acc = lax.fori_loop(0, nch, body, jnp.zeros((8, Q), jnp.float32))
        return jnp.sum(acc, axis=0, keepdims=True)

    def key_to_float(u):
        key = u ^ _INT_MIN
        bits = jnp.where(key >= 0, key, key ^ 0x7FFFFFFF)
        return lax.bitcast_convert_type(bits, jnp.float32)

    def bit_body(i, carry):
        prefix, cnt_best = carry
        cand_u = prefix | lax.shift_left(jnp.int32(1), 31 - i)
        cand = key_to_float(cand_u)
        cnt = count_rows(lambda blk: blk >= cand)
        take = cnt >= k_row
        return jnp.where(take, cand_u, prefix), jnp.where(take, cnt, cnt_best)

    prefix, cnt_best = lax.fori_loop(
        0, 32, bit_body, (jnp.zeros((1, Q), jnp.int32), k_row))
    thr = key_to_float(prefix)

    @pl.when(jnp.max(cnt_best - k_row) > 0.0)
    def _():
        need = k_row - count_rows(lambda blk: blk > thr)
        r_i = lax.broadcasted_iota(jnp.int32, (KC, KC), 0)
        c_i = lax.broadcasted_iota(jnp.int32, (KC, KC), 1)
        tri = jnp.where(c_i <= r_i, 1.0, 0.0).astype(jnp.bfloat16)

        def drop_body(c, seen):
            r0 = pl.multiple_of(c * KC, KC)
            blk = sc_ref[pl.ds(r0, KC), :]
            eq = blk == thr
            eq_f = jnp.where(eq, 1.0, 0.0)
            incl = jnp.dot(tri, eq_f.astype(jnp.bfloat16), preferred_element_type=jnp.float32)
            rank = seen + incl - eq_f
            sc_ref[pl.ds(r0, KC), :] = jnp.where(eq & (rank >= need), -jnp.inf, blk)
            return seen + jnp.sum(eq_f, axis=0, keepdims=True)

        lax.fori_loop(0, nch, drop_body, jnp.zeros((1, Q), jnp.float32))

    m_ref[...] = jnp.full(m_ref.shape, NEG_BIG, jnp.float32)
    acc_ref[...] = jnp.zeros(acc_ref.shape, jnp.float32)

    def att_chunk(c, carry):
        r0 = pl.multiple_of(c * KC, KC)
        ka = ka_ref[pl.ds(r0, KC), :]
        vt = vt_ref[c]
        mask_bias = jnp.where(sc_ref[pl.ds(r0, KC), :] >= thr, 0.0, NEG_BIG)
        for p in range(N_ATT_HEADS // 2):
            q_pair = jnp.concatenate(
                [qa_ref[:, (2 * p) * QK_AUG:(2 * p + 1) * QK_AUG],
                 qa_ref[:, (2 * p + 1) * QK_AUG:(2 * p + 2) * QK_AUG]], axis=0)
            lg = lax.dot_general(ka, q_pair, nt, preferred_element_type=jnp.float32)
            ps, alphas = [], []
            for hh in range(2):
                h = 2 * p + hh
                lgh = lg[:, hh * Q:(hh + 1) * Q] + mask_bias
                m_old = m_ref[h][0:1, :]
                m_new = jnp.maximum(m_old, jnp.max(lgh, axis=0, keepdims=True))
                alphas.append(jnp.exp(m_old - m_new))
                ps.append(jnp.exp(lgh - m_new).astype(jnp.bfloat16))
                m_ref[h] = jnp.broadcast_to(m_new, (8, Q))
            pv = jnp.dot(vt, jnp.concatenate(ps, axis=1), preferred_element_type=jnp.float32)
            acc_ref[p] = acc_ref[p] * jnp.concatenate(alphas, axis=1) + pv
        return carry

    lax.fori_loop(0, nch, att_chunk, 0)

    for h in range(N_ATT_HEADS):
        a = acc_ref[h // 2][:, (h % 2) * Q:(h % 2 + 1) * Q]
        out_t = a[0:ATT_HEAD_DIM, :] / a[ATT_HEAD_DIM:ATT_HEAD_DIM + 1, :]
        o_ref[:, h * ATT_HEAD_DIM:(h + 1) * ATT_HEAD_DIM] = out_t.T.astype(o_ref.dtype)


def _attn(qa, proj, widx, kd, ka, vt, B, S, n_sel):
    T = B * S
    nb = S // Q_BLOCK
    n_kc = S // KEY_CHUNK
    qrow = lambda b, j: (b * nb + j, 0)
    per_b = lambda b, j: (b, 0)
    return pl.pallas_call(
        functools.partial(_attn_kernel, n_sel),
        out_shape=jax.ShapeDtypeStruct((T, D_MODEL), jnp.bfloat16),
        grid=(B, nb),
        in_specs=[pl.BlockSpec((Q_BLOCK, N_ATT_HEADS * QK_AUG), qrow),
                  pl.BlockSpec((Q_BLOCK, 512), lambda b, j: (b * nb + j, C_QIDX // 512)),
                  pl.BlockSpec((Q_BLOCK, LANES), qrow),
                  pl.BlockSpec((S, LANES), per_b),
                  pl.BlockSpec((S, QK_AUG), per_b),
                  pl.BlockSpec((n_kc, VT_ROWS, KEY_CHUNK), lambda b, j: (b, 0, 0))],
        out_specs=pl.BlockSpec((Q_BLOCK, D_MODEL), qrow),
        scratch_shapes=[pltpu.VMEM((S, Q_BLOCK), jnp.float32),
                        pltpu.VMEM((IDX_HEADS // 2, 2 * Q_BLOCK, LANES), jnp.bfloat16),
                        pltpu.VMEM((N_ATT_HEADS, 8, Q_BLOCK), jnp.float32),
                        pltpu.VMEM((N_ATT_HEADS // 2, VT_ROWS, 2 * Q_BLOCK), jnp.float32)],
        compiler_params=_cparams(("parallel", "arbitrary")),
        name="attn",
    )(qa, proj, widx, kd, ka, vt)


def _ret_kernel(rq_ref, rk_ref, rv_ref, gate_ref, g_ref, o_ref, state_ref):
    C = RET_CHUNK

    @pl.when(pl.program_id(1) == 0)
    def _():
        state_ref[...] = jnp.zeros(state_ref.shape, jnp.float32)

    ii = lax.broadcasted_iota(jnp.int32, (C, C), 0)
    jj = lax.broadcasted_iota(jnp.int32, (C, C), 1)
    rel = (ii - jj).astype(jnp.float32)
    pos = lax.broadcasted_iota(jnp.int32, (C, 1), 0).astype(jnp.float32)
    for h in range(N_RET_HEADS):
        log_g = math.log(1.0 - 2.0 ** (-5.0 - h))
        decay_intra = jnp.where(rel >= 0, jnp.exp(rel * log_g), 0.0)
        q_decay = jnp.exp((pos + 1.0) * log_g)
        k_decay = jnp.exp((C - 1.0 - pos) * log_g)
        chunk_decay = math.exp(C * log_g)

        q = rq_ref[:, h * RET_QK_DIM:(h + 1) * RET_QK_DIM].astype(jnp.float32)
        k = rk_ref[:, h * RET_QK_DIM:(h + 1) * RET_QK_DIM].astype(jnp.float32) * (RET_QK_DIM ** -0.5)
        v = rv_ref[:, h * RET_V_DIM:(h + 1) * RET_V_DIM]
        state = state_ref[h]

        qk = lax.dot_general(q.astype(jnp.bfloat16), k.astype(jnp.bfloat16),
                             (((1,), (1,)), ((), ())), preferred_element_type=jnp.float32)
        intra = (qk * decay_intra).astype(jnp.bfloat16)
        o = (jnp.dot(intra, v, preferred_element_type=jnp.float32)
             + jnp.dot((q * q_decay).astype(jnp.bfloat16), state.astype(jnp.bfloat16),
                       preferred_element_type=jnp.float32))
        kd_t = (k * k_decay).T.astype(jnp.bfloat16)
        state_ref[h] = chunk_decay * state + jnp.dot(kd_t, v, preferred_element_type=jnp.float32)

        mu = jnp.mean(o, axis=-1, keepdims=True)
        var = jnp.mean(jnp.square(o - mu), axis=-1, keepdims=True)
        sl = slice(h * RET_V_DIM, (h + 1) * RET_V_DIM)
        y = (o - mu) * lax.rsqrt(var + EPS) * g_ref[:, sl]
        gate = gate_ref[:, sl].astype(jnp.float32)
        o_ref[:, sl] = (gate * jax.nn.sigmoid(gate) * y).astype(o_ref.dtype)


def _ret(proj, ret_g, B, S):
    T = B * S
    n = S // RET_CHUNK
    C = RET_CHUNK
    return pl.pallas_call(
        _ret_kernel,
        out_shape=jax.ShapeDtypeStruct((T, N_RET_HEADS * RET_V_DIM), jnp.bfloat16),
        grid=(B, n),
        in_specs=[pl.BlockSpec((C, 512), lambda b, i: (b * n + i, C_RQ // 512)),
                  pl.BlockSpec((C, 512), lambda b, i: (b * n + i, C_RK // 512)),
                  pl.BlockSpec((C, 1024), lambda b, i: (b * n + i, C_RV // 1024)),
                  pl.BlockSpec((C, 1024), lambda b, i: (b * n + i, C_RGATE // 1024)),
                  pl.BlockSpec((1, 1024), lambda b, i: (0, 0))],
        out_specs=pl.BlockSpec((C, 1024), lambda b, i: (b * n + i, 0)),
        scratch_shapes=[pltpu.VMEM((N_RET_HEADS, RET_QK_DIM, RET_V_DIM), jnp.float32)],
        compiler_params=_cparams(("parallel", "arbitrary")),
        name="ret",
    )(proj, proj, proj, proj, ret_g)


def _merge_kernel(att_ref, ret_ref, ga_ref, gr_ref, x_ref, wa_ref, wr_ref, wo_ref,
                  g2_ref, wrt_ref, brt_ref, h_ref, n2_ref, comb_ref):
    y_att = jnp.dot(att_ref[...], wa_ref[...], preferred_element_type=jnp.float32)
    y_ret = jnp.dot(ret_ref[...], wr_ref[...], preferred_element_type=jnp.float32)
    mixed = (jax.nn.sigmoid(ga_ref[...].astype(jnp.float32)) * y_att
             + jax.nn.sigmoid(gr_ref[...].astype(jnp.float32)) * y_ret)
    h = x_ref[...] + jnp.dot(mixed.astype(jnp.bfloat16), wo_ref[...],
                             preferred_element_type=jnp.float32)
    h_ref[...] = h
    ms = jnp.mean(h * h, axis=-1, keepdims=True)
    n2 = (h * lax.rsqrt(ms + EPS) * g2_ref[...]).astype(jnp.bfloat16)
    n2_ref[...] = n2

    logits = jnp.dot(n2, wrt_ref[...], preferred_element_type=jnp.float32) + brt_ref[...]
    lane = lax.broadcasted_iota(jnp.int32, logits.shape, 1)
    big = jnp.int32(ROUTER_W)
    is_g = lane < N_GROUPS
    g_max = jnp.max(jnp.where(is_g, logits, -jnp.inf), axis=-1, keepdims=True)
    g_sel = jnp.min(jnp.where(is_g & (logits == g_max), lane, big), axis=-1, keepdims=True)
    g_w = 1.0 / jnp.sum(jnp.where(is_g, jnp.exp(logits - g_max), 0.0), axis=-1, keepdims=True)
    lo = N_GROUPS + g_sel * EXPERTS_PER_GROUP
    in_grp = (lane >= lo) & (lane < lo + EXPERTS_PER_GROUP)
    v1 = jnp.max(jnp.where(in_grp, logits, -jnp.inf), axis=-1, keepdims=True)
    i1 = jnp.min(jnp.where(in_grp & (logits == v1), lane, big), axis=-1, keepdims=True)
    rest = in_grp & (lane != i1)
    v2 = jnp.max(jnp.where(rest, logits, -jnp.inf), axis=-1, keepdims=True)
    i2 = jnp.min(jnp.where(rest & (logits == v2), lane, big), axis=-1, keepdims=True)
    e2 = jnp.exp(v2 - v1)
    p1 = g_w / (1.0 + e2)
    p2 = p1 * e2
    comb_ref[...] = jnp.where(lane == i1, p1, 0.0) + jnp.where(lane == i2, p2, 0.0)


def _merge(att, gret, proj, x2, wa, wr, wo, g2, w_router, b_router, tm):
    T = x2.shape[0]
    row = lambda i: (i, 0)
    const = lambda i: (0, 0)
    return pl.pallas_call(
        _merge_kernel,
        out_shape=(jax.ShapeDtypeStruct((T, D_MODEL), jnp.float32),
                   jax.ShapeDtypeStruct((T, D_MODEL), jnp.bfloat16),
                   jax.ShapeDtypeStruct((T, ROUTER_W), jnp.float32)),
        grid=(T // tm,),
        in_specs=[pl.BlockSpec((tm, 1024), row),
                  pl.BlockSpec((tm, 1024), row),
                  pl.BlockSpec((tm, 1024), lambda i: (i, C_GATT // 1024)),
                  pl.BlockSpec((tm, 1024), lambda i: (i, C_GRET // 1024)),
                  pl.BlockSpec((tm, 1024), row),
                  pl.BlockSpec((1024, 1024), const),
                  pl.BlockSpec((1024, 1024), const),
                  pl.BlockSpec((1024, 1024), const),
                  pl.BlockSpec((1, 1024), const),
                  pl.BlockSpec((1024, ROUTER_W), const),
                  pl.BlockSpec((1, ROUTER_W), const)],
        out_specs=(pl.BlockSpec((tm, 1024), row),
                   pl.BlockSpec((tm, 1024), row),
                   pl.BlockSpec((tm, ROUTER_W), row)),
        compiler_params=_cparams(("parallel",)),
        name="merge",
    )(att, gret, proj, proj, x2, wa, wr, wo, g2, w_router, b_router)


def _moe_kernel(n2_ref, comb_ref, h_ref, wgu_ref, wd_ref, o_ref, acc_ref):
    g = pl.program_id(1)
    GF = EXPERTS_PER_GROUP * EXPERT_FF
    comb = comb_ref[...]
    lane = lax.broadcasted_iota(jnp.int32, comb.shape, 1)
    gu = jnp.dot(n2_ref[...], wgu_ref[0], preferred_element_type=jnp.float32)
    acts = []
    for e in range(EXPERTS_PER_GROUP):
        c = jnp.sum(jnp.where(lane == N_GROUPS + g * EXPERTS_PER_GROUP + e, comb, 0.0),
                    axis=-1, keepdims=True)
        hg = gu[:, e * EXPERT_FF:(e + 1) * EXPERT_FF]
        hu = gu[:, GF + e * EXPERT_FF:GF + (e + 1) * EXPERT_FF]
        acts.append((hg * jax.nn.sigmoid(hg) * hu * c).astype(jnp.bfloat16))
    y = jnp.dot(jnp.concatenate(acts, axis=1), wd_ref[0], preferred_element_type=jnp.float32)

    @pl.when(g == 0)
    def _():
        acc_ref[...] = y

    @pl.when(g > 0)
    def _():
        acc_ref[...] += y

    @pl.when(g == N_GROUPS - 1)
    def _():
        o_ref[...] = h_ref[...] + acc_ref[...]


def _moe(n2, comb, h1, w_gu, w_d, tm):
    T = n2.shape[0]
    GF = EXPERTS_PER_GROUP * EXPERT_FF
    row = lambda i, g: (i, 0)
    return pl.pallas_call(
        _moe_kernel,
        out_shape=jax.ShapeDtypeStruct((T, D_MODEL), jnp.float32),
        grid=(T // tm, N_GROUPS),
        in_specs=[pl.BlockSpec((tm, D_MODEL), row),
                  pl.BlockSpec((tm, ROUTER_W), row),
                  pl.BlockSpec((tm, D_MODEL), row),
                  pl.BlockSpec((1, D_MODEL, 2 * GF), lambda i, g: (g, 0, 0)),
                  pl.BlockSpec((1, GF, D_MODEL), lambda i, g: (g, 0, 0))],
        out_specs=pl.BlockSpec((tm, D_MODEL), row),
        scratch_shapes=[pltpu.VMEM((tm, D_MODEL), jnp.float32)],
        compiler_params=_cparams(("parallel", "arbitrary")),
        name="moe",
    )(n2, comb, h1, w_gu, w_d)


def _group_expert_weights(w_gate, w_up, w_down):
    G, E, F = N_GROUPS, EXPERTS_PER_GROUP, EXPERT_FF
    D = w_gate.shape[1]
    by_group = lambda w: w.reshape(G, E, D, F).transpose(0, 2, 1, 3).reshape(G, D, E * F)
    w_gu = jnp.concatenate([by_group(w_gate), by_group(w_up)], axis=-1)
    return w_gu, w_down.reshape(G, E * F, D)


def _relayout_w_in(w):
    pad = jnp.zeros((w.shape[0], C_RQ - (C_WIDX + IDX_HEADS)), w.dtype)
    return jnp.concatenate(
        [w[:, 0:1024],
         w[:, 1280:1792],
         w[:, 1024:1280],
         w[:, 1792:1864],
         pad,
         w[:, 1864:]], axis=1)


def _layer(h, norm1_g, w_in, kv_norm_g, w_kv_up, q_norm_g, k_norm_g, idx_k_norm_g,
           ret_norm_g, w_att_branch, w_ret_branch, w_out, norm2_g, w_group_router,
           b_group_router, w_expert_router, b_expert_router, w_exp_gate, w_exp_up, w_exp_down):
    B, S, D = h.shape
    assert D == D_MODEL and S % KEY_CHUNK == 0 and S % Q_BLOCK == 0 and S <= 65536
    T = B * S
    n_sel = min(TOPK_MAX, S // 4)
    tm = 512 if S % 512 == 0 else 256
    bf = jnp.bfloat16
    f32 = jnp.float32

    x2 = h.reshape(T, D)
    w_p = _relayout_w_in(w_in).astype(bf)
    proj = _proj(x2, norm1_g.reshape(1, D).astype(f32), w_p, tm, 1024)

    idx_g_pad = jnp.concatenate([idx_k_norm_g.astype(f32), jnp.zeros((LANES - IDX_DIM,), f32)])
    qa, ka, vt, kd, widx = _prep(
        proj, q_norm_g.reshape(1, -1).astype(f32), kv_norm_g.reshape(1, -1).astype(f32),
        w_kv_up.astype(bf), k_norm_g.reshape(1, -1).astype(f32), idx_g_pad.reshape(1, LANES), tm, S)

    att = _attn(qa, proj, widx, kd, ka, vt, B, S, n_sel)
    gret = _ret(proj, ret_norm_g.reshape(1, -1).astype(f32), B, S)

    w_router = jnp.concatenate(
        [w_group_router, w_expert_router,
         jnp.zeros((D, ROUTER_W - N_GROUPS - N_EXPERTS), w_group_router.dtype)], axis=1).astype(bf)
    b_router = jnp.concatenate(
        [b_group_router, b_expert_router,
         jnp.zeros((ROUTER_W - N_GROUPS - N_EXPERTS,), b_group_router.dtype)]).reshape(1, ROUTER_W)
    h1, n2, comb = _merge(att, gret, proj, x2, w_att_branch.astype(bf), w_ret_branch.astype(bf),
                          w_out.astype(bf), norm2_g.reshape(1, D).astype(f32),
                          w_router, b_router.astype(f32), tm)

    w_gu, w_d = _group_expert_weights(w_exp_gate.astype(bf), w_exp_up.astype(bf), w_exp_down.astype(bf))
    out = _moe(n2, comb, h1, w_gu, w_d, tm)
    return out.reshape(B, S, D)


def kernel(x, norm1_g, w_in, kv_norm_g, w_kv_up, q_norm_g, k_norm_g, idx_k_norm_g, ret_norm_g,
           w_att_branch, w_ret_branch, w_out, norm2_g, w_group_router, b_group_router,
           w_expert_router, b_expert_router, w_exp_gate, w_exp_up, w_exp_down):
    h = x
    for l in range(norm1_g.shape[0]):
        h = _layer(h, norm1_g[l], w_in[l], kv_norm_g[l], w_kv_up[l], q_norm_g[l], k_norm_g[l],
                   idx_k_norm_g[l], ret_norm_g[l], w_att_branch[l], w_ret_branch[l], w_out[l],
                   norm2_g[l], w_group_router[l], b_group_router[l], w_expert_router[l],
                   b_expert_router[l], w_exp_gate[l], w_exp_up[l], w_exp_down[l])
    return h
```

```python
import functools
import math

import jax
import jax.numpy as jnp
from jax import lax
from jax.experimental import pallas as pl
from jax.experimental.pallas import tpu as pltpu

D_MODEL = 1024
N_ATT_HEADS = 8
ATT_HEAD_DIM = 128
KV_LORA = 256
IDX_HEADS = 8
IDX_DIM = 64
TOPK_MAX = 256
Q_BLOCK = 128
N_RET_HEADS = 4
RET_QK_DIM = 128
RET_V_DIM = 256
RET_CHUNK = 128
N_GROUPS = 4
EXPERTS_PER_GROUP = 4
N_EXPERTS = N_GROUPS * EXPERTS_PER_GROUP
EXPERT_FF = 256
EPS = 1e-6

LANES = 128
KEY_CHUNK = 256
QK_AUG = 256
VT_ROWS = 144
PROJ_W = 7168
C_QATT, C_QIDX, C_CKV, C_KIDX, C_WIDX = 0, 1024, 1536, 1792, 1856
C_RQ, C_RK, C_RV, C_RGATE, C_GATT, C_GRET = 2048, 2560, 3072, 4096, 5120, 6144
ROUTER_W = 128
NEG_BIG = -1e30
VMEM_LIMIT = 56 * 1024 * 1024

_INT_MIN = -(2 ** 31)


def _cparams(sem):
    return pltpu.CompilerParams(dimension_semantics=sem, vmem_limit_bytes=VMEM_LIMIT)


def _proj_kernel(tn, x_ref, g_ref, w_ref, o_ref):
    x = x_ref[...]
    ms = jnp.mean(x * x, axis=-1, keepdims=True)
    n = (x * lax.rsqrt(ms + EPS) * g_ref[...]).astype(jnp.bfloat16)
    for c in range(PROJ_W // tn):
        o_ref[:, c * tn:(c + 1) * tn] = jnp.dot(
            n, w_ref[:, c * tn:(c + 1) * tn], preferred_element_type=jnp.float32).astype(o_ref.dtype)


def _proj(x2, g1, w_p, tm, tn):
    T = x2.shape[0]
    return pl.pallas_call(
        functools.partial(_proj_kernel, tn),
        out_shape=jax.ShapeDtypeStruct((T, PROJ_W), jnp.bfloat16),
        grid=(T // tm,),
        in_specs=[pl.BlockSpec((tm, D_MODEL), lambda i: (i, 0)),
                  pl.BlockSpec((1, D_MODEL), lambda i: (0, 0)),
                  pl.BlockSpec((D_MODEL, PROJ_W), lambda i: (0, 0))],
        out_specs=pl.BlockSpec((tm, PROJ_W), lambda i: (i, 0)),
        compiler_params=_cparams(("parallel",)),
        name="proj",
    )(x2, g1, w_p)


def _prep_kernel(S, q_ref, c_ref, qg_ref, kvg_ref, wkv_ref, kg_ref, ig_ref,
                 qa_ref, ka_ref, vt_ref, kd_ref, w_ref):
    tm = q_ref.shape[0]
    lane = lax.broadcasted_iota(jnp.int32, (tm, LANES), 1)

    qg = qg_ref[...]
    for h in range(N_ATT_HEADS):
        qh = q_ref[:, h * ATT_HEAD_DIM:(h + 1) * ATT_HEAD_DIM].astype(jnp.float32)
        ms = jnp.mean(qh * qh, axis=-1, keepdims=True)
        qa_ref[:, h * QK_AUG:h * QK_AUG + ATT_HEAD_DIM] = (
            qh * lax.rsqrt(ms + EPS) * qg * (ATT_HEAD_DIM ** -0.5)).astype(jnp.bfloat16)
        slope = 2.0 ** (-8.0 * (h + 1) / N_ATT_HEADS)
        qa_ref[:, h * QK_AUG + ATT_HEAD_DIM:(h + 1) * QK_AUG] = jnp.where(
            lane < 2, slope, 0.0).astype(jnp.bfloat16)

    c = c_ref[:, 0:KV_LORA].astype(jnp.float32)
    ms = jnp.mean(c * c, axis=-1, keepdims=True)
    cn = (c * lax.rsqrt(ms + EPS) * kvg_ref[...]).astype(jnp.bfloat16)
    kv = jnp.dot(cn, wkv_ref[...], preferred_element_type=jnp.float32)
    k = kv[:, :ATT_HEAD_DIM]
    ms = jnp.mean(k * k, axis=-1, keepdims=True)
    ka_ref[:, 0:ATT_HEAD_DIM] = (k * lax.rsqrt(ms + EPS) * kg_ref[...]).astype(jnp.bfloat16)
    base = (pl.program_id(0) % (S // tm)) * tm
    pos = base + lax.broadcasted_iota(jnp.int32, (tm, LANES), 0)
    pos_lo = pos & 255
    ka_ref[:, ATT_HEAD_DIM:] = jnp.where(
        lane == 0, pos - pos_lo, jnp.where(lane == 1, pos_lo, 0)).astype(jnp.float32).astype(jnp.bfloat16)
    v = kv[:, ATT_HEAD_DIM:]
    ones_row = jnp.where(lax.broadcasted_iota(jnp.int32, (VT_ROWS - ATT_HEAD_DIM, KEY_CHUNK), 0) == 0,
                         1.0, 0.0).astype(jnp.bfloat16)
    for u in range(tm // KEY_CHUNK):
        vt_ref[u, 0:ATT_HEAD_DIM, :] = v[u * KEY_CHUNK:(u + 1) * KEY_CHUNK, :].T.astype(jnp.bfloat16)
        vt_ref[u, ATT_HEAD_DIM:, :] = ones_row

    blk = c_ref[:, C_KIDX - C_CKV:C_KIDX - C_CKV + LANES].astype(jnp.float32)
    is_k = lane < IDX_DIM
    ms = jnp.sum(jnp.where(is_k, blk * blk, 0.0), axis=-1, keepdims=True) * (1.0 / IDX_DIM)
    kn = jnp.where(is_k, blk * lax.rsqrt(ms + EPS) * ig_ref[...], 0.0)
    kd_ref[...] = (kn + pltpu.roll(kn, IDX_DIM, 1)).astype(jnp.bfloat16)
    w_ref[...] = blk * ((IDX_HEADS ** -0.5) * (IDX_DIM ** -0.5))


def _prep(proj, q_norm_g, kv_norm_g, w_kv_up, k_norm_g, idx_g_pad, tm, S):
    T = proj.shape[0]
    row = lambda i: (i, 0)
    const = lambda i: (0, 0)
    return pl.pallas_call(
        functools.partial(_prep_kernel, S),
        out_shape=(jax.ShapeDtypeStruct((T, N_ATT_HEADS * QK_AUG), jnp.bfloat16),
                   jax.ShapeDtypeStruct((T, QK_AUG), jnp.bfloat16),
                   jax.ShapeDtypeStruct((T // KEY_CHUNK, VT_ROWS, KEY_CHUNK), jnp.bfloat16),
                   jax.ShapeDtypeStruct((T, LANES), jnp.bfloat16),
                   jax.ShapeDtypeStruct((T, LANES), jnp.float32)),
        grid=(T // tm,),
        in_specs=[pl.BlockSpec((tm, 1024), lambda i: (i, C_QATT // 1024)),
                  pl.BlockSpec((tm, 512), lambda i: (i, C_CKV // 512)),
                  pl.BlockSpec((1, ATT_HEAD_DIM), const),
                  pl.BlockSpec((1, KV_LORA), const),
                  pl.BlockSpec((KV_LORA, 2 * ATT_HEAD_DIM), const),
                  pl.BlockSpec((1, ATT_HEAD_DIM), const),
                  pl.BlockSpec((1, LANES), const)],
        out_specs=(pl.BlockSpec((tm, N_ATT_HEADS * QK_AUG), row),
                   pl.BlockSpec((tm, QK_AUG), row),
                   pl.BlockSpec((tm // KEY_CHUNK, VT_ROWS, KEY_CHUNK), lambda i: (i, 0, 0)),
                   pl.BlockSpec((tm, LANES), row),
                   pl.BlockSpec((tm, LANES), row)),
        compiler_params=_cparams(("parallel",)),
        name="prep",
    )(proj, proj, q_norm_g, kv_norm_g, w_kv_up, k_norm_g, idx_g_pad)


def _attn_kernel(n_sel, qa_ref, qi_ref, w_ref, kd_ref, ka_ref, vt_ref, o_ref,
                 sc_ref, sc16_ref, pre_ref, cnt_ref, qs_ref, m_ref, acc_ref):
    j = pl.program_id(1)
    Q, KC = Q_BLOCK, KEY_CHUNK
    nch = (j * Q + Q + KC - 1) // KC
    q_pos = j * Q + lax.broadcasted_iota(jnp.int32, (1, Q), 1)
    row_in_chunk = lax.broadcasted_iota(jnp.int32, (KC, Q), 0)
    nt = (((1,), (1,)), ((), ()))

    lane = lax.broadcasted_iota(jnp.int32, (Q, LANES), 1)
    for p in range(IDX_HEADS // 2):
        qp = qi_ref[:, p * LANES:(p + 1) * LANES].astype(jnp.float32)
        qs_ref[p, 0:Q, :] = jnp.where(lane < IDX_DIM, qp, 0.0).astype(jnp.bfloat16)
        qs_ref[p, Q:2 * Q, :] = jnp.where(lane >= IDX_DIM, qp, 0.0).astype(jnp.bfloat16)
    w_t = w_ref[...].T

    def trunc16(x):
        return lax.bitcast_convert_type(lax.bitcast_convert_type(x, jnp.int32) & -65536, jnp.float32)

    def score_chunk(c, carry):
        r0 = pl.multiple_of(c * KC, KC)
        kd = kd_ref[pl.ds(r0, KC), :]
        s = jnp.zeros((KC, Q), jnp.float32)
        for p in range(IDX_HEADS // 2):
            d = lax.dot_general(kd, qs_ref[p], nt, preferred_element_type=jnp.float32)
            h0 = IDX_DIM + 2 * p
            s = (s + w_t[h0:h0 + 1, :] * jnp.maximum(d[:, :Q], 0.0)
                 + w_t[h0 + 1:h0 + 2, :] * jnp.maximum(d[:, Q:], 0.0))
        s = jnp.where(r0 + row_in_chunk <= q_pos, s, -jnp.inf)
        sc_ref[pl.ds(r0, KC), :] = s
        sc16_ref[pl.ds(r0, KC), :] = trunc16(s).astype(jnp.bfloat16)
        return carry

    lax.fori_loop(0, nch, score_chunk, 0)

    k_row = jnp.minimum(q_pos + 1, n_sel).astype(jnp.float32)

    def tree_sum(parts):
        while len(parts) > 1:
            odd = parts[-1:] if len(parts) % 2 else []
            parts = [a + b for a, b in zip(parts[0::2], parts[1::2])] + odd
        return parts[0]

    def count_rows(pred):
        def body(c, acc):
            r0 = pl.multiple_of(c * KC, KC)
            hit = jnp.where(pred(sc_ref[pl.ds(r0, KC), :]), 1.0, 0.0)
            return acc + tree_sum([hit[u * 8:(u + 1) * 8, :] for u in range(KC // 8)])

        acc = lax.fori_loop(0, nch, body, jnp.zeros((8, Q), jnp.float32))
        return jnp.sum(acc, axis=0, keepdims=True)

    def key_to_float(u):
        key = u ^ _INT_MIN
        bits = jnp.where(key >= 0, key, key ^ 0x7FFFFFFF)
        return lax.bitcast_convert_type(bits, jnp.float32)

    def bisect(n):
        one = jnp.ones((16, Q), jnp.bfloat16)
        zero = jnp.zeros((16, Q), jnp.bfloat16)

        def count16(cand):
            cand_b = jnp.broadcast_to(cand, (16, Q)).astype(jnp.bfloat16)
            parts = [jnp.where(sc16_ref[r:r + 16, :] >= cand_b, one, zero)
                     for r in range(0, n * KC, 16)]
            acc = tree_sum(parts)
            return jnp.sum(acc.astype(jnp.float32), axis=0, keepdims=True)

        def count32(cand):
            cand_b = jnp.broadcast_to(cand, (8, Q))
            parts = [jnp.where(sc_ref[r:r + 8, :] >= cand_b, 1.0, 0.0) for r in range(0, n * KC, 8)]
            return jnp.sum(tree_sum(parts), axis=0, keepdims=True)

        def bit_body(wide, i, carry):
            prefix, cnt_best = carry
            cand_u = prefix | lax.shift_left(jnp.int32(1), 31 - i)
            cand = key_to_float(cand_u)
            cnt = count32(cand) if wide else count16(trunc16(cand))
            take = cnt >= k_row
            return jnp.where(take, cand_u, prefix), jnp.where(take, cnt, cnt_best)

        carry = lax.fori_loop(0, 16, functools.partial(bit_body, False),
                              (jnp.zeros((1, Q), jnp.int32), k_row))
        prefix, cnt_best = lax.fori_loop(16, 32, functools.partial(bit_body, True), carry)
        pre_ref[...] = jnp.broadcast_to(prefix, (8, Q))
        cnt_ref[...] = jnp.broadcast_to(cnt_best, (8, Q))

    for n in range(1, sc_ref.shape[0] // KC + 1):
        pl.when(nch == n)(functools.partial(bisect, n))
    prefix = pre_ref[0:1, :]
    cnt_best = cnt_ref[0:1, :]
    thr = key_to_float(prefix)

    @pl.when(jnp.max(cnt_best - k_row) > 0.0)
    def _():
        need = k_row - count_rows(lambda blk: blk > thr)
        r_i = lax.broadcasted_iota(jnp.int32, (KC, KC), 0)
        c_i = lax.broadcasted_iota(jnp.int32, (KC, KC), 1)
        tri = jnp.where(c_i <= r_i, 1.0, 0.0).astype(jnp.bfloat16)

        def drop_body(c, seen):
            r0 = pl.multiple_of(c * KC, KC)
            blk = sc_ref[pl.ds(r0, KC), :]
            eq = blk == thr
            eq_f = jnp.where(eq, 1.0, 0.0)
            incl = jnp.dot(tri, eq_f.astype(jnp.bfloat16), preferred_element_type=jnp.float32)
            rank = seen + incl - eq_f
            sc_ref[pl.ds(r0, KC), :] = jnp.where(eq & (rank >= need), -jnp.inf, blk)
            return seen + jnp.sum(eq_f, axis=0, keepdims=True)

        lax.fori_loop(0, nch, drop_body, jnp.zeros((1, Q), jnp.float32))

    m_ref[...] = jnp.full(m_ref.shape, NEG_BIG, jnp.float32)
    acc_ref[...] = jnp.zeros(acc_ref.shape, jnp.float32)

    def att_chunk(c, carry):
        r0 = pl.multiple_of(c * KC, KC)
        ka = ka_ref[pl.ds(r0, KC), :]
        vt = vt_ref[c]
        mask_bias = jnp.where(sc_ref[pl.ds(r0, KC), :] >= thr, 0.0, NEG_BIG)
        for p in range(N_ATT_HEADS // 2):
            q_pair = jnp.concatenate(
                [qa_ref[:, (2 * p) * QK_AUG:(2 * p + 1) * QK_AUG],
                 qa_ref[:, (2 * p + 1) * QK_AUG:(2 * p + 2) * QK_AUG]], axis=0)
            lg = lax.dot_general(ka, q_pair, nt, preferred_element_type=jnp.float32)
            ps, alphas = [], []
            for hh in range(2):
                h = 2 * p + hh
                lgh = lg[:, hh * Q:(hh + 1) * Q] + mask_bias
                m_old = m_ref[h][0:1, :]
                m_new = jnp.maximum(m_old, jnp.max(lgh, axis=0, keepdims=True))
                alphas.append(jnp.exp(m_old - m_new))
                ps.append(jnp.exp(lgh - m_new).astype(jnp.bfloat16))
                m_ref[h] = jnp.broadcast_to(m_new, (8, Q))
            pv = jnp.dot(vt, jnp.concatenate(ps, axis=1), preferred_element_type=jnp.float32)
            acc_ref[p] = acc_ref[p] * jnp.concatenate(alphas, axis=1) + pv
        return carry

    lax.fori_loop(0, nch, att_chunk, 0)

    for h in range(N_ATT_HEADS):
        a = acc_ref[h // 2][:, (h % 2) * Q:(h % 2 + 1) * Q]
        out_t = a[0:ATT_HEAD_DIM, :] / a[ATT_HEAD_DIM:ATT_HEAD_DIM + 1, :]
        o_ref[:, h * ATT_HEAD_DIM:(h + 1) * ATT_HEAD_DIM] = out_t.T.astype(o_ref.dtype)


def _attn(qa, proj, widx, kd, ka, vt, B, S, n_sel):
    T = B * S
    nb = S // Q_BLOCK
    n_kc = S // KEY_CHUNK
    qrow = lambda b, j: (b * nb + j, 0)
    per_b = lambda b, j: (b, 0)
    return pl.pallas_call(
        functools.partial(_attn_kernel, n_sel),
        out_shape=jax.ShapeDtypeStruct((T, D_MODEL), jnp.bfloat16),
        grid=(B, nb),
        in_specs=[pl.BlockSpec((Q_BLOCK, N_ATT_HEADS * QK_AUG), qrow),
                  pl.BlockSpec((Q_BLOCK, 512), lambda b, j: (b * nb + j, C_QIDX // 512)),
                  pl.BlockSpec((Q_BLOCK, LANES), qrow),
                  pl.BlockSpec((S, LANES), per_b),
                  pl.BlockSpec((S, QK_AUG), per_b),
                  pl.BlockSpec((n_kc, VT_ROWS, KEY_CHUNK), lambda b, j: (b, 0, 0))],
        out_specs=pl.BlockSpec((Q_BLOCK, D_MODEL), qrow),
        scratch_shapes=[pltpu.VMEM((S, Q_BLOCK), jnp.float32),
                        pltpu.VMEM((S, Q_BLOCK), jnp.bfloat16),
                        pltpu.VMEM((8, Q_BLOCK), jnp.int32),
                        pltpu.VMEM((8, Q_BLOCK), jnp.float32),
                        pltpu.VMEM((IDX_HEADS // 2, 2 * Q_BLOCK, LANES), jnp.bfloat16),
                        pltpu.VMEM((N_ATT_HEADS, 8, Q_BLOCK), jnp.float32),
                        pltpu.VMEM((N_ATT_HEADS // 2, VT_ROWS, 2 * Q_BLOCK), jnp.float32)],
        compiler_params=_cparams(("parallel", "arbitrary")),
        name="attn",
    )(qa, proj, widx, kd, ka, vt)


def _ret_kernel(rq_ref, rk_ref, rv_ref, gate_ref, g_ref, o_ref, state_ref):
    C = RET_CHUNK

    @pl.when(pl.program_id(1) == 0)
    def _():
        state_ref[...] = jnp.zeros(state_ref.shape, jnp.float32)

    ii = lax.broadcasted_iota(jnp.int32, (C, C), 0)
    jj = lax.broadcasted_iota(jnp.int32, (C, C), 1)
    rel = (ii - jj).astype(jnp.float32)
    pos = lax.broadcasted_iota(jnp.int32, (C, 1), 0).astype(jnp.float32)
    for h in range(N_RET_HEADS):
        log_g = math.log(1.0 - 2.0 ** (-5.0 - h))
        decay_intra = jnp.where(rel >= 0, jnp.exp(rel * log_g), 0.0)
        q_decay = jnp.exp((pos + 1.0) * log_g)
        k_decay = jnp.exp((C - 1.0 - pos) * log_g)
        chunk_decay = math.exp(C * log_g)

        q = rq_ref[:, h * RET_QK_DIM:(h + 1) * RET_QK_DIM].astype(jnp.float32)
        k = rk_ref[:, h * RET_QK_DIM:(h + 1) * RET_QK_DIM].astype(jnp.float32) * (RET_QK_DIM ** -0.5)
        v = rv_ref[:, h * RET_V_DIM:(h + 1) * RET_V_DIM]
        state = state_ref[h]

        qk = lax.dot_general(q.astype(jnp.bfloat16), k.astype(jnp.bfloat16),
                             (((1,), (1,)), ((), ())), preferred_element_type=jnp.float32)
        intra = (qk * decay_intra).astype(jnp.bfloat16)
        o = (jnp.dot(intra, v, preferred_element_type=jnp.float32)
             + jnp.dot((q * q_decay).astype(jnp.bfloat16), state.astype(jnp.bfloat16),
                       preferred_element_type=jnp.float32))
        kd_t = (k * k_decay).T.astype(jnp.bfloat16)
        state_ref[h] = chunk_decay * state + jnp.dot(kd_t, v, preferred_element_type=jnp.float32)

        mu = jnp.mean(o, axis=-1, keepdims=True)
        var = jnp.mean(jnp.square(o - mu), axis=-1, keepdims=True)
        sl = slice(h * RET_V_DIM, (h + 1) * RET_V_DIM)
        y = (o - mu) * lax.rsqrt(var + EPS) * g_ref[:, sl]
        gate = gate_ref[:, sl].astype(jnp.float32)
        o_ref[:, sl] = (gate * jax.nn.sigmoid(gate) * y).astype(o_ref.dtype)


def _ret(proj, ret_g, B, S):
    T = B * S
    n = S // RET_CHUNK
    C = RET_CHUNK
    return pl.pallas_call(
        _ret_kernel,
        out_shape=jax.ShapeDtypeStruct((T, N_RET_HEADS * RET_V_DIM), jnp.bfloat16),
        grid=(B, n),
        in_specs=[pl.BlockSpec((C, 512), lambda b, i: (b * n + i, C_RQ // 512)),
                  pl.BlockSpec((C, 512), lambda b, i: (b * n + i, C_RK // 512)),
                  pl.BlockSpec((C, 1024), lambda b, i: (b * n + i, C_RV // 1024)),
                  pl.BlockSpec((C, 1024), lambda b, i: (b * n + i, C_RGATE // 1024)),
                  pl.BlockSpec((1, 1024), lambda b, i: (0, 0))],
        out_specs=pl.BlockSpec((C, 1024), lambda b, i: (b * n + i, 0)),
        scratch_shapes=[pltpu.VMEM((N_RET_HEADS, RET_QK_DIM, RET_V_DIM), jnp.float32)],
        compiler_params=_cparams(("parallel", "arbitrary")),
        name="ret",
    )(proj, proj, proj, proj, ret_g)


def _merge_kernel(att_ref, ret_ref, ga_ref, gr_ref, x_ref, wa_ref, wr_ref, wo_ref,
                  g2_ref, wrt_ref, brt_ref, h_ref, n2_ref, comb_ref):
    y_att = jnp.dot(att_ref[...], wa_ref[...], preferred_element_type=jnp.float32)
    y_ret = jnp.dot(ret_ref[...], wr_ref[...], preferred_element_type=jnp.float32)
    mixed = (jax.nn.sigmoid(ga_ref[...].astype(jnp.float32)) * y_att
             + jax.nn.sigmoid(gr_ref[...].astype(jnp.float32)) * y_ret)
    h = x_ref[...] + jnp.dot(mixed.astype(jnp.bfloat16), wo_ref[...],
                             preferred_element_type=jnp.float32)
    h_ref[...] = h
    ms = jnp.mean(h * h, axis=-1, keepdims=True)
    n2 = (h * lax.rsqrt(ms + EPS) * g2_ref[...]).astype(jnp.bfloat16)
    n2_ref[...] = n2

    logits = jnp.dot(n2, wrt_ref[...], preferred_element_type=jnp.float32) + brt_ref[...]
    lane = lax.broadcasted_iota(jnp.int32, logits.shape, 1)
    big = jnp.int32(ROUTER_W)
    is_g = lane < N_GROUPS
    g_max = jnp.max(jnp.where(is_g, logits, -jnp.inf), axis=-1, keepdims=True)
    g_sel = jnp.min(jnp.where(is_g & (logits == g_max), lane, big), axis=-1, keepdims=True)
    g_w = 1.0 / jnp.sum(jnp.where(is_g, jnp.exp(logits - g_max), 0.0), axis=-1, keepdims=True)
    lo = N_GROUPS + g_sel * EXPERTS_PER_GROUP
    in_grp = (lane >= lo) & (lane < lo + EXPERTS_PER_GROUP)
    v1 = jnp.max(jnp.where(in_grp, logits, -jnp.inf), axis=-1, keepdims=True)
    i1 = jnp.min(jnp.where(in_grp & (logits == v1), lane, big), axis=-1, keepdims=True)
    rest = in_grp & (lane != i1)
    v2 = jnp.max(jnp.where(rest, logits, -jnp.inf), axis=-1, keepdims=True)
    i2 = jnp.min(jnp.where(rest & (logits == v2), lane, big), axis=-1, keepdims=True)
    e2 = jnp.exp(v2 - v1)
    p1 = g_w / (1.0 + e2)
    p2 = p1 * e2
    comb_ref[...] = jnp.where(lane == i1, p1, 0.0) + jnp.where(lane == i2, p2, 0.0)


def _merge(att, gret, proj, x2, wa, wr, wo, g2, w_router, b_router, tm):
    T = x2.shape[0]
    row = lambda i: (i, 0)
    const = lambda i: (0, 0)
    return pl.pallas_call(
        _merge_kernel,
        out_shape=(jax.ShapeDtypeStruct((T, D_MODEL), jnp.float32),
                   jax.ShapeDtypeStruct((T, D_MODEL), jnp.bfloat16),
                   jax.ShapeDtypeStruct((T, ROUTER_W), jnp.float32)),
        grid=(T // tm,),
        in_specs=[pl.BlockSpec((tm, 1024), row),
                  pl.BlockSpec((tm, 1024), row),
                  pl.BlockSpec((tm, 1024), lambda i: (i, C_GATT // 1024)),
                  pl.BlockSpec((tm, 1024), lambda i: (i, C_GRET // 1024)),
                  pl.BlockSpec((tm, 1024), row),
                  pl.BlockSpec((1024, 1024), const),
                  pl.BlockSpec((1024, 1024), const),
                  pl.BlockSpec((1024, 1024), const),
                  pl.BlockSpec((1, 1024), const),
                  pl.BlockSpec((1024, ROUTER_W), const),
                  pl.BlockSpec((1, ROUTER_W), const)],
        out_specs=(pl.BlockSpec((tm, 1024), row),
                   pl.BlockSpec((tm, 1024), row),
                   pl.BlockSpec((tm, ROUTER_W), row)),
        compiler_params=_cparams(("parallel",)),
        name="merge",
    )(att, gret, proj, proj, x2, wa, wr, wo, g2, w_router, b_router)


def _moe_kernel(n2_ref, comb_ref, h_ref, wg_ref, wu_ref, wd_ref, o_ref, acc_ref):
    g = pl.program_id(1)
    n2 = n2_ref[...]
    comb = comb_ref[...]
    lane = lax.broadcasted_iota(jnp.int32, comb.shape, 1)
    acts = []
    for e in range(EXPERTS_PER_GROUP):
        c = jnp.sum(jnp.where(lane == N_GROUPS + g * EXPERTS_PER_GROUP + e, comb, 0.0),
                    axis=-1, keepdims=True)
        hg = jnp.dot(n2, wg_ref[e], preferred_element_type=jnp.float32)
        hu = jnp.dot(n2, wu_ref[e], preferred_element_type=jnp.float32)
        acts.append((hg * jax.nn.sigmoid(hg) * hu * c).astype(jnp.bfloat16))
    y = jnp.dot(jnp.concatenate(acts, axis=1), wd_ref[0], preferred_element_type=jnp.float32)

    @pl.when(g == 0)
    def _():
        acc_ref[...] = y

    @pl.when(g > 0)
    def _():
        acc_ref[...] += y

    @pl.when(g == N_GROUPS - 1)
    def _():
        o_ref[...] = h_ref[...] + acc_ref[...]


def _moe(n2, comb, h1, w_g, w_u, w_d, tm):
    T = n2.shape[0]
    E, F = EXPERTS_PER_GROUP, EXPERT_FF
    row = lambda i, g: (i, 0)
    grp = lambda i, g: (g, 0, 0)
    return pl.pallas_call(
        _moe_kernel,
        out_shape=jax.ShapeDtypeStruct((T, D_MODEL), jnp.float32),
        grid=(T // tm, N_GROUPS),
        in_specs=[pl.BlockSpec((tm, D_MODEL), row),
                  pl.BlockSpec((tm, ROUTER_W), row),
                  pl.BlockSpec((tm, D_MODEL), row),
                  pl.BlockSpec((E, D_MODEL, F), grp),
                  pl.BlockSpec((E, D_MODEL, F), grp),
                  pl.BlockSpec((1, E * F, D_MODEL), grp)],
        out_specs=pl.BlockSpec((tm, D_MODEL), row),
        scratch_shapes=[pltpu.VMEM((tm, D_MODEL), jnp.float32)],
        compiler_params=_cparams(("parallel", "arbitrary")),
        name="moe",
    )(n2, comb, h1, w_g, w_u, w_d)


def _relayout_w_in(w):
    pad = jnp.zeros((w.shape[0], C_RQ - (C_WIDX + IDX_HEADS)), w.dtype)
    return jnp.concatenate(
        [w[:, 0:1024],
         w[:, 1280:1792],
         w[:, 1024:1280],
         w[:, 1792:1864],
         pad,
         w[:, 1864:]], axis=1)


def _layer(h, norm1_g, w_in, kv_norm_g, w_kv_up, q_norm_g, k_norm_g, idx_k_norm_g,
           ret_norm_g, w_att_branch, w_ret_branch, w_out, norm2_g, w_group_router,
           b_group_router, w_expert_router, b_expert_router, w_exp_gate, w_exp_up, w_exp_down):
    B, S, D = h.shape
    assert D == D_MODEL and S % KEY_CHUNK == 0 and S % Q_BLOCK == 0 and S <= 65536
    T = B * S
    n_sel = min(TOPK_MAX, S // 4)
    tm = 512 if S % 512 == 0 else 256
    bf = jnp.bfloat16
    f32 = jnp.float32

    x2 = h.reshape(T, D)
    w_p = _relayout_w_in(w_in).astype(bf)
    proj = _proj(x2, norm1_g.reshape(1, D).astype(f32), w_p, tm, 1024)

    idx_g_pad = jnp.concatenate([idx_k_norm_g.astype(f32), jnp.zeros((LANES - IDX_DIM,), f32)])
    qa, ka, vt, kd, widx = _prep(
        proj, q_norm_g.reshape(1, -1).astype(f32), kv_norm_g.reshape(1, -1).astype(f32),
        w_kv_up.astype(bf), k_norm_g.reshape(1, -1).astype(f32), idx_g_pad.reshape(1, LANES), tm, S)

    att = _attn(qa, proj, widx, kd, ka, vt, B, S, n_sel)
    gret = _ret(proj, ret_norm_g.reshape(1, -1).astype(f32), B, S)

    w_router = jnp.concatenate(
        [w_group_router, w_expert_router,
         jnp.zeros((D, ROUTER_W - N_GROUPS - N_EXPERTS), w_group_router.dtype)], axis=1).astype(bf)
    b_router = jnp.concatenate(
        [b_group_router, b_expert_router,
         jnp.zeros((ROUTER_W - N_GROUPS - N_EXPERTS,), b_group_router.dtype)]).reshape(1, ROUTER_W)
    h1, n2, comb = _merge(att, gret, proj, x2, w_att_branch.astype(bf), w_ret_branch.astype(bf),
                          w_out.astype(bf), norm2_g.reshape(1, D).astype(f32),
                          w_router, b_router.astype(f32), tm)

    w_d = w_exp_down.astype(bf).reshape(N_GROUPS, EXPERTS_PER_GROUP * EXPERT_FF, D)
    out = _moe(n2, comb, h1, w_exp_gate.astype(bf), w_exp_up.astype(bf), w_d, tm)
    return out.reshape(B, S, D)


def kernel(x, norm1_g, w_in, kv_norm_g, w_kv_up, q_norm_g, k_norm_g, idx_k_norm_g, ret_norm_g,
           w_att_branch, w_ret_branch, w_out, norm2_g, w_group_router, b_group_router,
           w_expert_router, b_expert_router, w_exp_gate, w_exp_up, w_exp_down):
    h = x
    for l in range(norm1_g.shape[0]):
        h = _layer(h, norm1_g[l], w_in[l], kv_norm_g[l], w_kv_up[l], q_norm_g[l], k_norm_g[l],
                   idx_k_norm_g[l], ret_norm_g[l], w_att_branch[l], w_ret_branch[l], w_out[l],
                   norm2_g[l], w_group_router[l], b_group_router[l], w_expert_router[l],
                   b_expert_router[l], w_exp_gate[l], w_exp_up[l], w_exp_down[l])
    return h
```

```python
import functools
import math

import jax
import jax.numpy as jnp
from jax import lax
from jax.experimental import pallas as pl
from jax.experimental.pallas import tpu as pltpu

D_MODEL = 1024
N_ATT_HEADS = 8
ATT_HEAD_DIM = 128
KV_LORA = 256
IDX_HEADS = 8
IDX_DIM = 64
TOPK_MAX = 256
Q_BLOCK = 128
N_RET_HEADS = 4
RET_QK_DIM = 128
RET_V_DIM = 256
RET_CHUNK = 128
N_GROUPS = 4
EXPERTS_PER_GROUP = 4
N_EXPERTS = N_GROUPS * EXPERTS_PER_GROUP
EXPERT_FF = 256
EPS = 1e-6

LANES = 128
KEY_CHUNK = 256
QK_AUG = 256
N_SWEEP_ACC = 4
VT_ROWS = 144
PROJ_W = 7168
C_QATT, C_QIDX, C_CKV, C_KIDX, C_WIDX = 0, 1024, 1536, 1792, 1856
C_RQ, C_RK, C_RV, C_RGATE, C_GATT, C_GRET = 2048, 2560, 3072, 4096, 5120, 6144
ROUTER_W = 128
NEG_BIG = -1e30
VMEM_LIMIT = 56 * 1024 * 1024

_INT_MIN = -(2 ** 31)


def _cparams(sem):
    return pltpu.CompilerParams(dimension_semantics=sem, vmem_limit_bytes=VMEM_LIMIT)


def _proj_kernel(tn, x_ref, g_ref, w_ref, o_ref):
    x = x_ref[...]
    ms = jnp.mean(x * x, axis=-1, keepdims=True)
    n = (x * lax.rsqrt(ms + EPS) * g_ref[...]).astype(jnp.bfloat16)
    for c in range(PROJ_W // tn):
        o_ref[:, c * tn:(c + 1) * tn] = jnp.dot(
            n, w_ref[:, c * tn:(c + 1) * tn], preferred_element_type=jnp.float32).astype(o_ref.dtype)


def _proj(x2, g1, w_p, tm, tn):
    T = x2.shape[0]
    return pl.pallas_call(
        functools.partial(_proj_kernel, tn),
        out_shape=jax.ShapeDtypeStruct((T, PROJ_W), jnp.bfloat16),
        grid=(T // tm,),
        in_specs=[pl.BlockSpec((tm, D_MODEL), lambda i: (i, 0)),
                  pl.BlockSpec((1, D_MODEL), lambda i: (0, 0)),
                  pl.BlockSpec((D_MODEL, PROJ_W), lambda i: (0, 0))],
        out_specs=pl.BlockSpec((tm, PROJ_W), lambda i: (i, 0)),
        compiler_params=_cparams(("parallel",)),
        name="proj",
    )(x2, g1, w_p)


def _prep_kernel(S, q_ref, c_ref, qg_ref, kvg_ref, wkv_ref, kg_ref, ig_ref,
                 qa_ref, ka_ref, vt_ref, kd_ref, w_ref):
    tm = q_ref.shape[0]
    lane = lax.broadcasted_iota(jnp.int32, (tm, LANES), 1)

    qg = qg_ref[...]
    for h in range(N_ATT_HEADS):
        qh = q_ref[:, h * ATT_HEAD_DIM:(h + 1) * ATT_HEAD_DIM].astype(jnp.float32)
        ms = jnp.mean(qh * qh, axis=-1, keepdims=True)
        qa_ref[:, h * QK_AUG:h * QK_AUG + ATT_HEAD_DIM] = (
            qh * lax.rsqrt(ms + EPS) * qg * (ATT_HEAD_DIM ** -0.5)).astype(jnp.bfloat16)
        slope = 2.0 ** (-8.0 * (h + 1) / N_ATT_HEADS)
        qa_ref[:, h * QK_AUG + ATT_HEAD_DIM:(h + 1) * QK_AUG] = jnp.where(
            lane < 2, slope, 0.0).astype(jnp.bfloat16)

    c = c_ref[:, 0:KV_LORA].astype(jnp.float32)
    ms = jnp.mean(c * c, axis=-1, keepdims=True)
    cn = (c * lax.rsqrt(ms + EPS) * kvg_ref[...]).astype(jnp.bfloat16)
    kv = jnp.dot(cn, wkv_ref[...], preferred_element_type=jnp.float32)
    k = kv[:, :ATT_HEAD_DIM]
    ms = jnp.mean(k * k, axis=-1, keepdims=True)
    ka_ref[:, 0:ATT_HEAD_DIM] = (k * lax.rsqrt(ms + EPS) * kg_ref[...]).astype(jnp.bfloat16)
    base = (pl.program_id(0) % (S // tm)) * tm
    pos = base + lax.broadcasted_iota(jnp.int32, (tm, LANES), 0)
    pos_lo = pos & 255
    ka_ref[:, ATT_HEAD_DIM:] = jnp.where(
        lane == 0, pos - pos_lo, jnp.where(lane == 1, pos_lo, 0)).astype(jnp.float32).astype(jnp.bfloat16)
    v = kv[:, ATT_HEAD_DIM:]
    ones_row = jnp.where(lax.broadcasted_iota(jnp.int32, (VT_ROWS - ATT_HEAD_DIM, KEY_CHUNK), 0) == 0,
                         1.0, 0.0).astype(jnp.bfloat16)
    for u in range(tm // KEY_CHUNK):
        vt_ref[u, 0:ATT_HEAD_DIM, :] = v[u * KEY_CHUNK:(u + 1) * KEY_CHUNK, :].T.astype(jnp.bfloat16)
        vt_ref[u, ATT_HEAD_DIM:, :] = ones_row

    blk = c_ref[:, C_KIDX - C_CKV:C_KIDX - C_CKV + LANES].astype(jnp.float32)
    is_k = lane < IDX_DIM
    ms = jnp.sum(jnp.where(is_k, blk * blk, 0.0), axis=-1, keepdims=True) * (1.0 / IDX_DIM)
    kn = jnp.where(is_k, blk * lax.rsqrt(ms + EPS) * ig_ref[...], 0.0)
    kd_ref[...] = (kn + pltpu.roll(kn, IDX_DIM, 1)).astype(jnp.bfloat16)
    w_ref[...] = blk * ((IDX_HEADS ** -0.5) * (IDX_DIM ** -0.5))


def _prep(proj, q_norm_g, kv_norm_g, w_kv_up, k_norm_g, idx_g_pad, tm, S):
    T = proj.shape[0]
    row = lambda i: (i, 0)
    const = lambda i: (0, 0)
    return pl.pallas_call(
        functools.partial(_prep_kernel, S),
        out_shape=(jax.ShapeDtypeStruct((T, N_ATT_HEADS * QK_AUG), jnp.bfloat16),
                   jax.ShapeDtypeStruct((T, QK_AUG), jnp.bfloat16),
                   jax.ShapeDtypeStruct((T // KEY_CHUNK, VT_ROWS, KEY_CHUNK), jnp.bfloat16),
                   jax.ShapeDtypeStruct((T, LANES), jnp.bfloat16),
                   jax.ShapeDtypeStruct((T, LANES), jnp.float32)),
        grid=(T // tm,),
        in_specs=[pl.BlockSpec((tm, 1024), lambda i: (i, C_QATT // 1024)),
                  pl.BlockSpec((tm, 512), lambda i: (i, C_CKV // 512)),
                  pl.BlockSpec((1, ATT_HEAD_DIM), const),
                  pl.BlockSpec((1, KV_LORA), const),
                  pl.BlockSpec((KV_LORA, 2 * ATT_HEAD_DIM), const),
                  pl.BlockSpec((1, ATT_HEAD_DIM), const),
                  pl.BlockSpec((1, LANES), const)],
        out_specs=(pl.BlockSpec((tm, N_ATT_HEADS * QK_AUG), row),
                   pl.BlockSpec((tm, QK_AUG), row),
                   pl.BlockSpec((tm // KEY_CHUNK, VT_ROWS, KEY_CHUNK), lambda i: (i, 0, 0)),
                   pl.BlockSpec((tm, LANES), row),
                   pl.BlockSpec((tm, LANES), row)),
        compiler_params=_cparams(("parallel",)),
        name="prep",
    )(proj, proj, q_norm_g, kv_norm_g, w_kv_up, k_norm_g, idx_g_pad)


def _attn_kernel(n_sel, qa_ref, qi_ref, w_ref, kd_ref, ka_ref, vt_ref, o_ref,
                 sc_ref, pre_ref, cnt_ref, qs_ref, m_ref, acc_ref):
    j = pl.program_id(1)
    Q, KC = Q_BLOCK, KEY_CHUNK
    nch = (j * Q + Q + KC - 1) // KC
    q_pos = j * Q + lax.broadcasted_iota(jnp.int32, (1, Q), 1)
    row_in_chunk = lax.broadcasted_iota(jnp.int32, (KC, Q), 0)
    nt = (((1,), (1,)), ((), ()))

    lane = lax.broadcasted_iota(jnp.int32, (Q, LANES), 1)
    for p in range(IDX_HEADS // 2):
        qp = qi_ref[:, p * LANES:(p + 1) * LANES].astype(jnp.float32)
        qs_ref[p, 0:Q, :] = jnp.where(lane < IDX_DIM, qp, 0.0).astype(jnp.bfloat16)
        qs_ref[p, Q:2 * Q, :] = jnp.where(lane >= IDX_DIM, qp, 0.0).astype(jnp.bfloat16)
    w_t = w_ref[...].T

    def score_chunk(c, carry):
        r0 = pl.multiple_of(c * KC, KC)
        kd = kd_ref[pl.ds(r0, KC), :]
        s = jnp.zeros((KC, Q), jnp.float32)
        for p in range(IDX_HEADS // 2):
            d = lax.dot_general(kd, qs_ref[p], nt, preferred_element_type=jnp.float32)
            h0 = IDX_DIM + 2 * p
            s = (s + w_t[h0:h0 + 1, :] * jnp.maximum(d[:, :Q], 0.0)
                 + w_t[h0 + 1:h0 + 2, :] * jnp.maximum(d[:, Q:], 0.0))
        sc_ref[pl.ds(r0, KC), :] = jnp.where(r0 + row_in_chunk <= q_pos, s, -jnp.inf)
        return carry

    lax.fori_loop(0, nch, score_chunk, 0)

    k_row = jnp.minimum(q_pos + 1, n_sel).astype(jnp.float32)

    def tree_sum(parts):
        while len(parts) > 1:
            odd = parts[-1:] if len(parts) % 2 else []
            parts = [a + b for a, b in zip(parts[0::2], parts[1::2])] + odd
        return parts[0]

    def count_rows(pred):
        def body(c, acc):
            r0 = pl.multiple_of(c * KC, KC)
            hit = jnp.where(pred(sc_ref[pl.ds(r0, KC), :]), 1.0, 0.0)
            return acc + tree_sum([hit[u * 8:(u + 1) * 8, :] for u in range(KC // 8)])

        acc = lax.fori_loop(0, nch, body, jnp.zeros((8, Q), jnp.float32))
        return jnp.sum(acc, axis=0, keepdims=True)

    def key_to_float(u):
        key = u ^ _INT_MIN
        bits = jnp.where(key >= 0, key, key ^ 0x7FFFFFFF)
        return lax.bitcast_convert_type(bits, jnp.float32)

    def bisect(n):
        def count_ge(cand):
            cand_b = jnp.broadcast_to(cand, (8, Q))
            accs = [jnp.zeros((8, Q), jnp.float32) for _ in range(N_SWEEP_ACC)]
            for u, r in enumerate(range(0, n * KC, 8)):
                accs[u % N_SWEEP_ACC] = accs[u % N_SWEEP_ACC] + jnp.where(sc_ref[r:r + 8, :] >= cand_b, 1.0, 0.0)
            return jnp.sum(tree_sum(accs), axis=0, keepdims=True)

        def bit_body(i, carry):
            prefix, cnt_best = carry
            cand_u = prefix | lax.shift_left(jnp.int32(1), 31 - i)
            cnt = count_ge(key_to_float(cand_u))
            take = cnt >= k_row
            return jnp.where(take, cand_u, prefix), jnp.where(take, cnt, cnt_best)

        prefix, cnt_best = lax.fori_loop(0, 32, bit_body, (jnp.zeros((1, Q), jnp.int32), k_row))
        pre_ref[...] = jnp.broadcast_to(prefix, (8, Q))
        cnt_ref[...] = jnp.broadcast_to(cnt_best, (8, Q))

    for n in range(1, sc_ref.shape[0] // KC + 1):
        pl.when(nch == n)(functools.partial(bisect, n))
    prefix = pre_ref[0:1, :]
    cnt_best = cnt_ref[0:1, :]
    thr = key_to_float(prefix)

    @pl.when(jnp.max(cnt_best - k_row) > 0.0)
    def _():
        need = k_row - count_rows(lambda blk: blk > thr)
        r_i = lax.broadcasted_iota(jnp.int32, (KC, KC), 0)
        c_i = lax.broadcasted_iota(jnp.int32, (KC, KC), 1)
        tri = jnp.where(c_i <= r_i, 1.0, 0.0).astype(jnp.bfloat16)

        def drop_body(c, seen):
            r0 = pl.multiple_of(c * KC, KC)
            blk = sc_ref[pl.ds(r0, KC), :]
            eq = blk == thr
            eq_f = jnp.where(eq, 1.0, 0.0)
            incl = jnp.dot(tri, eq_f.astype(jnp.bfloat16), preferred_element_type=jnp.float32)
            rank = seen + incl - eq_f
            sc_ref[pl.ds(r0, KC), :] = jnp.where(eq & (rank >= need), -jnp.inf, blk)
            return seen + jnp.sum(eq_f, axis=0, keepdims=True)

        lax.fori_loop(0, nch, drop_body, jnp.zeros((1, Q), jnp.float32))

    m_ref[...] = jnp.full(m_ref.shape, NEG_BIG, jnp.float32)
    acc_ref[...] = jnp.zeros(acc_ref.shape, jnp.float32)

    def att_chunk(c, carry):
        r0 = pl.multiple_of(c * KC, KC)
        ka = ka_ref[pl.ds(r0, KC), :]
        vt = vt_ref[c]
        mask_bias = jnp.where(sc_ref[pl.ds(r0, KC), :] >= thr, 0.0, NEG_BIG)
        for p in range(N_ATT_HEADS // 2):
            q_pair = jnp.concatenate(
                [qa_ref[:, (2 * p) * QK_AUG:(2 * p + 1) * QK_AUG],
                 qa_ref[:, (2 * p + 1) * QK_AUG:(2 * p + 2) * QK_AUG]], axis=0)
            lg = lax.dot_general(ka, q_pair, nt, preferred_element_type=jnp.float32)
            ps, alphas = [], []
            for hh in range(2):
                h = 2 * p + hh
                lgh = lg[:, hh * Q:(hh + 1) * Q] + mask_bias
                m_old = m_ref[h][0:1, :]
                m_new = jnp.maximum(m_old, jnp.max(lgh, axis=0, keepdims=True))
                alphas.append(jnp.exp(m_old - m_new))
                ps.append(jnp.exp(lgh - m_new).astype(jnp.bfloat16))
                m_ref[h] = jnp.broadcast_to(m_new, (8, Q))
            pv = jnp.dot(vt, jnp.concatenate(ps, axis=1), preferred_element_type=jnp.float32)
            acc_ref[p] = acc_ref[p] * jnp.concatenate(alphas, axis=1) + pv
        return carry

    lax.fori_loop(0, nch, att_chunk, 0)

    for h in range(N_ATT_HEADS):
        a = acc_ref[h // 2][:, (h % 2) * Q:(h % 2 + 1) * Q]
        out_t = a[0:ATT_HEAD_DIM, :] / a[ATT_HEAD_DIM:ATT_HEAD_DIM + 1, :]
        o_ref[:, h * ATT_HEAD_DIM:(h + 1) * ATT_HEAD_DIM] = out_t.T.astype(o_ref.dtype)


def _attn(qa, proj, widx, kd, ka, vt, B, S, n_sel):
    T = B * S
    nb = S // Q_BLOCK
    n_kc = S // KEY_CHUNK
    qrow = lambda b, j: (b * nb + j, 0)
    per_b = lambda b, j: (b, 0)
    return pl.pallas_call(
        functools.partial(_attn_kernel, n_sel),
        out_shape=jax.ShapeDtypeStruct((T, D_MODEL), jnp.bfloat16),
        grid=(B, nb),
        in_specs=[pl.BlockSpec((Q_BLOCK, N_ATT_HEADS * QK_AUG), qrow),
                  pl.BlockSpec((Q_BLOCK, 512), lambda b, j: (b * nb + j, C_QIDX // 512)),
                  pl.BlockSpec((Q_BLOCK, LANES), qrow),
                  pl.BlockSpec((S, LANES), per_b),
                  pl.BlockSpec((S, QK_AUG), per_b),
                  pl.BlockSpec((n_kc, VT_ROWS, KEY_CHUNK), lambda b, j: (b, 0, 0))],
        out_specs=pl.BlockSpec((Q_BLOCK, D_MODEL), qrow),
        scratch_shapes=[pltpu.VMEM((S, Q_BLOCK), jnp.float32),
                        pltpu.VMEM((8, Q_BLOCK), jnp.int32),
                        pltpu.VMEM((8, Q_BLOCK), jnp.float32),
                        pltpu.VMEM((IDX_HEADS // 2, 2 * Q_BLOCK, LANES), jnp.bfloat16),
                        pltpu.VMEM((N_ATT_HEADS, 8, Q_BLOCK), jnp.float32),
                        pltpu.VMEM((N_ATT_HEADS // 2, VT_ROWS, 2 * Q_BLOCK), jnp.float32)],
        compiler_params=_cparams(("parallel", "arbitrary")),
        name="attn",
    )(qa, proj, widx, kd, ka, vt)


def _ret_kernel(rq_ref, rk_ref, rv_ref, gate_ref, g_ref, o_ref, state_ref):
    C = RET_CHUNK

    @pl.when(pl.program_id(1) == 0)
    def _():
        state_ref[...] = jnp.zeros(state_ref.shape, jnp.float32)

    ii = lax.broadcasted_iota(jnp.int32, (C, C), 0)
    jj = lax.broadcasted_iota(jnp.int32, (C, C), 1)
    rel = (ii - jj).astype(jnp.float32)
    pos = lax.broadcasted_iota(jnp.int32, (C, 1), 0).astype(jnp.float32)
    for h in range(N_RET_HEADS):
        log_g = math.log(1.0 - 2.0 ** (-5.0 - h))
        decay_intra = jnp.where(rel >= 0, jnp.exp(rel * log_g), 0.0)
        q_decay = jnp.exp((pos + 1.0) * log_g)
        k_decay = jnp.exp((C - 1.0 - pos) * log_g)
        chunk_decay = math.exp(C * log_g)

        q = rq_ref[:, h * RET_QK_DIM:(h + 1) * RET_QK_DIM].astype(jnp.float32)
        k = rk_ref[:, h * RET_QK_DIM:(h + 1) * RET_QK_DIM].astype(jnp.float32) * (RET_QK_DIM ** -0.5)
        v = rv_ref[:, h * RET_V_DIM:(h + 1) * RET_V_DIM]
        state = state_ref[h]

        qk = lax.dot_general(q.astype(jnp.bfloat16), k.astype(jnp.bfloat16),
                             (((1,), (1,)), ((), ())), preferred_element_type=jnp.float32)
        intra = (qk * decay_intra).astype(jnp.bfloat16)
        o = (jnp.dot(intra, v, preferred_element_type=jnp.float32)
             + jnp.dot((q * q_decay).astype(jnp.bfloat16), state.astype(jnp.bfloat16),
                       preferred_element_type=jnp.float32))
        kd_t = (k * k_decay).T.astype(jnp.bfloat16)
        state_ref[h] = chunk_decay * state + jnp.dot(kd_t, v, preferred_element_type=jnp.float32)

        mu = jnp.mean(o, axis=-1, keepdims=True)
        var = jnp.mean(jnp.square(o - mu), axis=-1, keepdims=True)
        sl = slice(h * RET_V_DIM, (h + 1) * RET_V_DIM)
        y = (o - mu) * lax.rsqrt(var + EPS) * g_ref[:, sl]
        gate = gate_ref[:, sl].astype(jnp.float32)
        o_ref[:, sl] = (gate * jax.nn.sigmoid(gate) * y).astype(o_ref.dtype)


def _ret(proj, ret_g, B, S):
    T = B * S
    n = S // RET_CHUNK
    C = RET_CHUNK
    return pl.pallas_call(
        _ret_kernel,
        out_shape=jax.ShapeDtypeStruct((T, N_RET_HEADS * RET_V_DIM), jnp.bfloat16),
        grid=(B, n),
        in_specs=[pl.BlockSpec((C, 512), lambda b, i: (b * n + i, C_RQ // 512)),
                  pl.BlockSpec((C, 512), lambda b, i: (b * n + i, C_RK // 512)),
                  pl.BlockSpec((C, 1024), lambda b, i: (b * n + i, C_RV // 1024)),
                  pl.BlockSpec((C, 1024), lambda b, i: (b * n + i, C_RGATE // 1024)),
                  pl.BlockSpec((1, 1024), lambda b, i: (0, 0))],
        out_specs=pl.BlockSpec((C, 1024), lambda b, i: (b * n + i, 0)),
        scratch_shapes=[pltpu.VMEM((N_RET_HEADS, RET_QK_DIM, RET_V_DIM), jnp.float32)],
        compiler_params=_cparams(("parallel", "arbitrary")),
        name="ret",
    )(proj, proj, proj, proj, ret_g)


def _merge_kernel(att_ref, ret_ref, ga_ref, gr_ref, x_ref, wa_ref, wr_ref, wo_ref,
                  g2_ref, wrt_ref, brt_ref, h_ref, n2_ref, comb_ref):
    y_att = jnp.dot(att_ref[...], wa_ref[...], preferred_element_type=jnp.float32)
    y_ret = jnp.dot(ret_ref[...], wr_ref[...], preferred_element_type=jnp.float32)
    mixed = (jax.nn.sigmoid(ga_ref[...].astype(jnp.float32)) * y_att
             + jax.nn.sigmoid(gr_ref[...].astype(jnp.float32)) * y_ret)
    h = x_ref[...] + jnp.dot(mixed.astype(jnp.bfloat16), wo_ref[...],
                             preferred_element_type=jnp.float32)
    h_ref[...] = h
    ms = jnp.mean(h * h, axis=-1, keepdims=True)
    n2 = (h * lax.rsqrt(ms + EPS) * g2_ref[...]).astype(jnp.bfloat16)
    n2_ref[...] = n2

    logits = jnp.dot(n2, wrt_ref[...], preferred_element_type=jnp.float32) + brt_ref[...]
    lane = lax.broadcasted_iota(jnp.int32, logits.shape, 1)
    big = jnp.int32(ROUTER_W)
    is_g = lane < N_GROUPS
    g_max = jnp.max(jnp.where(is_g, logits, -jnp.inf), axis=-1, keepdims=True)
    g_sel = jnp.min(jnp.where(is_g & (logits == g_max), lane, big), axis=-1, keepdims=True)
    g_w = 1.0 / jnp.sum(jnp.where(is_g, jnp.exp(logits - g_max), 0.0), axis=-1, keepdims=True)
    lo = N_GROUPS + g_sel * EXPERTS_PER_GROUP
    in_grp = (lane >= lo) & (lane < lo + EXPERTS_PER_GROUP)
    v1 = jnp.max(jnp.where(in_grp, logits, -jnp.inf), axis=-1, keepdims=True)
    i1 = jnp.min(jnp.where(in_grp & (logits == v1), lane, big), axis=-1, keepdims=True)
    rest = in_grp & (lane != i1)
    v2 = jnp.max(jnp.where(rest, logits, -jnp.inf), axis=-1, keepdims=True)
    i2 = jnp.min(jnp.where(rest & (logits == v2), lane, big), axis=-1, keepdims=True)
    e2 = jnp.exp(v2 - v1)
    p1 = g_w / (1.0 + e2)
    p2 = p1 * e2
    comb_ref[...] = jnp.where(lane == i1, p1, 0.0) + jnp.where(lane == i2, p2, 0.0)


def _merge(att, gret, proj, x2, wa, wr, wo, g2, w_router, b_router, tm):
    T = x2.shape[0]
    row = lambda i: (i, 0)
    const = lambda i: (0, 0)
    return pl.pallas_call(
        _merge_kernel,
        out_shape=(jax.ShapeDtypeStruct((T, D_MODEL), jnp.float32),
                   jax.ShapeDtypeStruct((T, D_MODEL), jnp.bfloat16),
                   jax.ShapeDtypeStruct((T, ROUTER_W), jnp.float32)),
        grid=(T // tm,),
        in_specs=[pl.BlockSpec((tm, 1024), row),
                  pl.BlockSpec((tm, 1024), row),
                  pl.BlockSpec((tm, 1024), lambda i: (i, C_GATT // 1024)),
                  pl.BlockSpec((tm, 1024), lambda i: (i, C_GRET // 1024)),
                  pl.BlockSpec((tm, 1024), row),
                  pl.BlockSpec((1024, 1024), const),
                  pl.BlockSpec((1024, 1024), const),
                  pl.BlockSpec((1024, 1024), const),
                  pl.BlockSpec((1, 1024), const),
                  pl.BlockSpec((1024, ROUTER_W), const),
                  pl.BlockSpec((1, ROUTER_W), const)],
        out_specs=(pl.BlockSpec((tm, 1024), row),
                   pl.BlockSpec((tm, 1024), row),
                   pl.BlockSpec((tm, ROUTER_W), row)),
        compiler_params=_cparams(("parallel",)),
        name="merge",
    )(att, gret, proj, proj, x2, wa, wr, wo, g2, w_router, b_router)


def _moe_kernel(n2_ref, comb_ref, h_ref, wg_ref, wu_ref, wd_ref, o_ref, acc_ref):
    g = pl.program_id(1)
    n2 = n2_ref[...]
    comb = comb_ref[...]
    lane = lax.broadcasted_iota(jnp.int32, comb.shape, 1)
    acts = []
    for e in range(EXPERTS_PER_GROUP):
        c = jnp.sum(jnp.where(lane == N_GROUPS + g * EXPERTS_PER_GROUP + e, comb, 0.0),
                    axis=-1, keepdims=True)
        hg = jnp.dot(n2, wg_ref[e], preferred_element_type=jnp.float32)
        hu = jnp.dot(n2, wu_ref[e], preferred_element_type=jnp.float32)
        acts.append((hg * jax.nn.sigmoid(hg) * hu * c).astype(jnp.bfloat16))
    y = jnp.dot(jnp.concatenate(acts, axis=1), wd_ref[0], preferred_element_type=jnp.float32)

    @pl.when(g == 0)
    def _():
        acc_ref[...] = y

    @pl.when(g > 0)
    def _():
        acc_ref[...] += y

    @pl.when(g == N_GROUPS - 1)
    def _():
        o_ref[...] = h_ref[...] + acc_ref[...]


def _moe(n2, comb, h1, w_g, w_u, w_d, tm):
    T = n2.shape[0]
    E, F = EXPERTS_PER_GROUP, EXPERT_FF
    row = lambda i, g: (i, 0)
    grp = lambda i, g: (g, 0, 0)
    return pl.pallas_call(
        _moe_kernel,
        out_shape=jax.ShapeDtypeStruct((T, D_MODEL), jnp.float32),
        grid=(T // tm, N_GROUPS),
        in_specs=[pl.BlockSpec((tm, D_MODEL), row),
                  pl.BlockSpec((tm, ROUTER_W), row),
                  pl.BlockSpec((tm, D_MODEL), row),
                  pl.BlockSpec((E, D_MODEL, F), grp),
                  pl.BlockSpec((E, D_MODEL, F), grp),
                  pl.BlockSpec((1, E * F, D_MODEL), grp)],
        out_specs=pl.BlockSpec((tm, D_MODEL), row),
        scratch_shapes=[pltpu.VMEM((tm, D_MODEL), jnp.float32)],
        compiler_params=_cparams(("parallel", "arbitrary")),
        name="moe",
    )(n2, comb, h1, w_g, w_u, w_d)


def _relayout_w_in(w):
    pad = jnp.zeros((w.shape[0], C_RQ - (C_WIDX + IDX_HEADS)), w.dtype)
    return jnp.concatenate(
        [w[:, 0:1024],
         w[:, 1280:1792],
         w[:, 1024:1280],
         w[:, 1792:1864],
         pad,
         w[:, 1864:]], axis=1)


def _layer(h, norm1_g, w_in, kv_norm_g, w_kv_up, q_norm_g, k_norm_g, idx_k_norm_g,
           ret_norm_g, w_att_branch, w_ret_branch, w_out, norm2_g, w_group_router,
           b_group_router, w_expert_router, b_expert_router, w_exp_gate, w_exp_up, w_exp_down):
    B, S, D = h.shape
    assert D == D_MODEL and S % KEY_CHUNK == 0 and S % Q_BLOCK == 0 and S <= 65536
    T = B * S
    n_sel = min(TOPK_MAX, S // 4)
    tm = 512 if S % 512 == 0 else 256
    bf = jnp.bfloat16
    f32 = jnp.float32

    x2 = h.reshape(T, D)
    w_p = _relayout_w_in(w_in).astype(bf)
    proj = _proj(x2, norm1_g.reshape(1, D).astype(f32), w_p, tm, 1024)

    idx_g_pad = jnp.concatenate([idx_k_norm_g.astype(f32), jnp.zeros((LANES - IDX_DIM,), f32)])
    qa, ka, vt, kd, widx = _prep(
        proj, q_norm_g.reshape(1, -1).astype(f32), kv_norm_g.reshape(1, -1).astype(f32),
        w_kv_up.astype(bf), k_norm_g.reshape(1, -1).astype(f32), idx_g_pad.reshape(1, LANES), tm, S)

    att = _attn(qa, proj, widx, kd, ka, vt, B, S, n_sel)
    gret = _ret(proj, ret_norm_g.reshape(1, -1).astype(f32), B, S)

    w_router = jnp.concatenate(
        [w_group_router, w_expert_router,
         jnp.zeros((D, ROUTER_W - N_GROUPS - N_EXPERTS), w_group_router.dtype)], axis=1).astype(bf)
    b_router = jnp.concatenate(
        [b_group_router, b_expert_router,
         jnp.zeros((ROUTER_W - N_GROUPS - N_EXPERTS,), b_group_router.dtype)]).reshape(1, ROUTER_W)
    h1, n2, comb = _merge(att, gret, proj, x2, w_att_branch.astype(bf), w_ret_branch.astype(bf),
                          w_out.astype(bf), norm2_g.reshape(1, D).astype(f32),
                          w_router, b_router.astype(f32), tm)

    w_d = w_exp_down.astype(bf).reshape(N_GROUPS, EXPERTS_PER_GROUP * EXPERT_FF, D)
    out = _moe(n2, comb, h1, w_exp_gate.astype(bf), w_exp_up.astype(bf), w_d, tm)
    return out.reshape(B, S, D)


def kernel(x, norm1_g, w_in, kv_norm_g, w_kv_up, q_norm_g, k_norm_g, idx_k_norm_g, ret_norm_g,
           w_att_branch, w_ret_branch, w_out, norm2_g, w_group_router, b_group_router,
           w_expert_router, b_expert_router, w_exp_gate, w_exp_up, w_exp_down):
    h = x
    for l in range(norm1_g.shape[0]):
        h = _layer(h, norm1_g[l], w_in[l], kv_norm_g[l], w_kv_up[l], q_norm_g[l], k_norm_g[l],
                   idx_k_norm_g[l], ret_norm_g[l], w_att_branch[l], w_ret_branch[l], w_out[l],
                   norm2_g[l], w_group_router[l], b_group_router[l], w_expert_router[l],
                   b_expert_router[l], w_exp_gate[l], w_exp_up[l], w_exp_down[l])
    return h
```

```python
import functools
import math

import jax
import jax.numpy as jnp
from jax import lax
from jax.experimental import pallas as pl
from jax.experimental.pallas import tpu as pltpu

D_MODEL = 1024
N_ATT_HEADS = 8
ATT_HEAD_DIM = 128
KV_LORA = 256
IDX_HEADS = 8
IDX_DIM = 64
TOPK_MAX = 256
Q_BLOCK = 128
ATTN_Q = 256
N_RET_HEADS = 4
RET_QK_DIM = 128
RET_V_DIM = 256
RET_CHUNK = 128
N_GROUPS = 4
EXPERTS_PER_GROUP = 4
N_EXPERTS = N_GROUPS * EXPERTS_PER_GROUP
EXPERT_FF = 256
EPS = 1e-6

LANES = 128
KEY_CHUNK = 256
QK_AUG = 256
RET_BATCHES = 4
N_SWEEP_ACC = 4
VT_ROWS = 144
PROJ_W = 7168
C_QATT, C_QIDX, C_CKV, C_KIDX, C_WIDX = 0, 1024, 1536, 1792, 1856
C_RQ, C_RK, C_RV, C_RGATE, C_GATT, C_GRET = 2048, 2560, 3072, 4096, 5120, 6144
ROUTER_W = 128
NEG_BIG = -1e30
VMEM_LIMIT = 56 * 1024 * 1024

_INT_MIN = -(2 ** 31)


def _cparams(sem):
    return pltpu.CompilerParams(dimension_semantics=sem, vmem_limit_bytes=VMEM_LIMIT)


def _proj_kernel(tn, x_ref, g_ref, w_ref, o_ref):
    x = x_ref[...]
    ms = jnp.mean(x * x, axis=-1, keepdims=True)
    n = (x * lax.rsqrt(ms + EPS) * g_ref[...]).astype(jnp.bfloat16)
    for c in range(PROJ_W // tn):
        o_ref[:, c * tn:(c + 1) * tn] = jnp.dot(
            n, w_ref[:, c * tn:(c + 1) * tn], preferred_element_type=jnp.float32).astype(o_ref.dtype)


def _proj(x2, g1, w_p, tm, tn):
    T = x2.shape[0]
    return pl.pallas_call(
        functools.partial(_proj_kernel, tn),
        out_shape=jax.ShapeDtypeStruct((T, PROJ_W), jnp.bfloat16),
        grid=(T // tm,),
        in_specs=[pl.BlockSpec((tm, D_MODEL), lambda i: (i, 0)),
                  pl.BlockSpec((1, D_MODEL), lambda i: (0, 0)),
                  pl.BlockSpec((D_MODEL, PROJ_W), lambda i: (0, 0))],
        out_specs=pl.BlockSpec((tm, PROJ_W), lambda i: (i, 0)),
        compiler_params=_cparams(("parallel",)),
        name="proj",
    )(x2, g1, w_p)


def _prep_kernel(S, q_ref, c_ref, qg_ref, kvg_ref, wkv_ref, kg_ref, ig_ref,
                 qa_ref, ka_ref, vt_ref, kd_ref, w_ref):
    tm = q_ref.shape[0]
    lane = lax.broadcasted_iota(jnp.int32, (tm, LANES), 1)

    qg = qg_ref[...]
    for h in range(N_ATT_HEADS):
        qh = q_ref[:, h * ATT_HEAD_DIM:(h + 1) * ATT_HEAD_DIM].astype(jnp.float32)
        ms = jnp.mean(qh * qh, axis=-1, keepdims=True)
        qa_ref[:, h * QK_AUG:h * QK_AUG + ATT_HEAD_DIM] = (
            qh * lax.rsqrt(ms + EPS) * qg * (ATT_HEAD_DIM ** -0.5)).astype(jnp.bfloat16)
        slope = 2.0 ** (-8.0 * (h + 1) / N_ATT_HEADS)
        qa_ref[:, h * QK_AUG + ATT_HEAD_DIM:(h + 1) * QK_AUG] = jnp.where(
            lane < 2, slope, 0.0).astype(jnp.bfloat16)

    c = c_ref[:, 0:KV_LORA].astype(jnp.float32)
    ms = jnp.mean(c * c, axis=-1, keepdims=True)
    cn = (c * lax.rsqrt(ms + EPS) * kvg_ref[...]).astype(jnp.bfloat16)
    kv = jnp.dot(cn, wkv_ref[...], preferred_element_type=jnp.float32)
    k = kv[:, :ATT_HEAD_DIM]
    ms = jnp.mean(k * k, axis=-1, keepdims=True)
    ka_ref[:, 0:ATT_HEAD_DIM] = (k * lax.rsqrt(ms + EPS) * kg_ref[...]).astype(jnp.bfloat16)
    base = (pl.program_id(0) % (S // tm)) * tm
    pos = base + lax.broadcasted_iota(jnp.int32, (tm, LANES), 0)
    pos_lo = pos & 255
    ka_ref[:, ATT_HEAD_DIM:] = jnp.where(
        lane == 0, pos - pos_lo, jnp.where(lane == 1, pos_lo, 0)).astype(jnp.float32).astype(jnp.bfloat16)
    v = kv[:, ATT_HEAD_DIM:]
    ones_row = jnp.where(lax.broadcasted_iota(jnp.int32, (VT_ROWS - ATT_HEAD_DIM, KEY_CHUNK), 0) == 0,
                         1.0, 0.0).astype(jnp.bfloat16)
    for u in range(tm // KEY_CHUNK):
        vt_ref[u, 0:ATT_HEAD_DIM, :] = v[u * KEY_CHUNK:(u + 1) * KEY_CHUNK, :].T.astype(jnp.bfloat16)
        vt_ref[u, ATT_HEAD_DIM:, :] = ones_row

    blk = c_ref[:, C_KIDX - C_CKV:C_KIDX - C_CKV + LANES].astype(jnp.float32)
    is_k = lane < IDX_DIM
    ms = jnp.sum(jnp.where(is_k, blk * blk, 0.0), axis=-1, keepdims=True) * (1.0 / IDX_DIM)
    kn = jnp.where(is_k, blk * lax.rsqrt(ms + EPS) * ig_ref[...], 0.0)
    kd_ref[...] = (kn + pltpu.roll(kn, IDX_DIM, 1)).astype(jnp.bfloat16)
    w_ref[...] = blk * ((IDX_HEADS ** -0.5) * (IDX_DIM ** -0.5))


def _prep(proj, q_norm_g, kv_norm_g, w_kv_up, k_norm_g, idx_g_pad, tm, S):
    T = proj.shape[0]
    row = lambda i: (i, 0)
    const = lambda i: (0, 0)
    return pl.pallas_call(
        functools.partial(_prep_kernel, S),
        out_shape=(jax.ShapeDtypeStruct((T, N_ATT_HEADS * QK_AUG), jnp.bfloat16),
                   jax.ShapeDtypeStruct((T, QK_AUG), jnp.bfloat16),
                   jax.ShapeDtypeStruct((T // KEY_CHUNK, VT_ROWS, KEY_CHUNK), jnp.bfloat16),
                   jax.ShapeDtypeStruct((T, LANES), jnp.bfloat16),
                   jax.ShapeDtypeStruct((T, LANES), jnp.float32)),
        grid=(T // tm,),
        in_specs=[pl.BlockSpec((tm, 1024), lambda i: (i, C_QATT // 1024)),
                  pl.BlockSpec((tm, 512), lambda i: (i, C_CKV // 512)),
                  pl.BlockSpec((1, ATT_HEAD_DIM), const),
                  pl.BlockSpec((1, KV_LORA), const),
                  pl.BlockSpec((KV_LORA, 2 * ATT_HEAD_DIM), const),
                  pl.BlockSpec((1, ATT_HEAD_DIM), const),
                  pl.BlockSpec((1, LANES), const)],
        out_specs=(pl.BlockSpec((tm, N_ATT_HEADS * QK_AUG), row),
                   pl.BlockSpec((tm, QK_AUG), row),
                   pl.BlockSpec((tm // KEY_CHUNK, VT_ROWS, KEY_CHUNK), lambda i: (i, 0, 0)),
                   pl.BlockSpec((tm, LANES), row),
                   pl.BlockSpec((tm, LANES), row)),
        compiler_params=_cparams(("parallel",)),
        name="prep",
    )(proj, proj, q_norm_g, kv_norm_g, w_kv_up, k_norm_g, idx_g_pad)


def _attn_kernel(n_sel, qa_ref, qi_ref, w_ref, kd_ref, ka_ref, vt_ref, o_ref,
                 sc_ref, pre_ref, cnt_ref, qs_ref, m_ref, acc_ref):
    j = pl.program_id(1)
    Q, KC = ATTN_Q, KEY_CHUNK
    nch = (j * Q + Q + KC - 1) // KC
    q_pos = j * Q + lax.broadcasted_iota(jnp.int32, (1, Q), 1)
    row_in_chunk = lax.broadcasted_iota(jnp.int32, (KC, Q), 0)
    nt = (((1,), (1,)), ((), ()))

    lane = lax.broadcasted_iota(jnp.int32, (Q, LANES), 1)
    for p in range(IDX_HEADS // 2):
        qp = qi_ref[:, p * LANES:(p + 1) * LANES].astype(jnp.float32)
        qs_ref[p, 0:Q, :] = jnp.where(lane < IDX_DIM, qp, 0.0).astype(jnp.bfloat16)
        qs_ref[p, Q:2 * Q, :] = jnp.where(lane >= IDX_DIM, qp, 0.0).astype(jnp.bfloat16)
    w_t = w_ref[...].T

    def score_chunk(c, carry):
        r0 = pl.multiple_of(c * KC, KC)
        kd = kd_ref[pl.ds(r0, KC), :]
        s = jnp.zeros((KC, Q), jnp.float32)
        for p in range(IDX_HEADS // 2):
            d = lax.dot_general(kd, qs_ref[p], nt, preferred_element_type=jnp.float32)
            h0 = IDX_DIM + 2 * p
            s = (s + w_t[h0:h0 + 1, :] * jnp.maximum(d[:, :Q], 0.0)
                 + w_t[h0 + 1:h0 + 2, :] * jnp.maximum(d[:, Q:], 0.0))
        sc_ref[pl.ds(r0, KC), :] = jnp.where(r0 + row_in_chunk <= q_pos, s, -jnp.inf)
        return carry

    lax.fori_loop(0, nch, score_chunk, 0)

    k_row = jnp.minimum(q_pos + 1, n_sel).astype(jnp.float32)

    def tree_sum(parts):
        while len(parts) > 1:
            odd = parts[-1:] if len(parts) % 2 else []
            parts = [a + b for a, b in zip(parts[0::2], parts[1::2])] + odd
        return parts[0]

    def count_rows(pred):
        def body(c, acc):
            r0 = pl.multiple_of(c * KC, KC)
            hit = jnp.where(pred(sc_ref[pl.ds(r0, KC), :]), 1.0, 0.0)
            return acc + tree_sum([hit[u * 8:(u + 1) * 8, :] for u in range(KC // 8)])

        acc = lax.fori_loop(0, nch, body, jnp.zeros((8, Q), jnp.float32))
        return jnp.sum(acc, axis=0, keepdims=True)

    def key_to_float(u):
        key = u ^ _INT_MIN
        bits = jnp.where(key >= 0, key, key ^ 0x7FFFFFFF)
        return lax.bitcast_convert_type(bits, jnp.float32)

    def bisect(n):
        def count_ge(cand):
            cand_b = jnp.broadcast_to(cand, (8, Q))
            accs = [jnp.zeros((8, Q), jnp.float32) for _ in range(N_SWEEP_ACC)]
            for u, r in enumerate(range(0, n * KC, 8)):
                accs[u % N_SWEEP_ACC] = accs[u % N_SWEEP_ACC] + jnp.where(sc_ref[r:r + 8, :] >= cand_b, 1.0, 0.0)
            return jnp.sum(tree_sum(accs), axis=0, keepdims=True)

        def bit_body(i, carry):
            prefix, cnt_best = carry
            cand_u = prefix | lax.shift_left(jnp.int32(1), 31 - i)
            cnt = count_ge(key_to_float(cand_u))
            take = cnt >= k_row
            return jnp.where(take, cand_u, prefix), jnp.where(take, cnt, cnt_best)

        prefix, cnt_best = lax.fori_loop(0, 32, bit_body, (jnp.zeros((1, Q), jnp.int32), k_row))
        pre_ref[...] = jnp.broadcast_to(prefix, (8, Q))
        cnt_ref[...] = jnp.broadcast_to(cnt_best, (8, Q))

    for n in range(1, sc_ref.shape[0] // KC + 1):
        pl.when(nch == n)(functools.partial(bisect, n))
    prefix = pre_ref[0:1, :]
    cnt_best = cnt_ref[0:1, :]
    thr = key_to_float(prefix)

    @pl.when(jnp.max(cnt_best - k_row) > 0.0)
    def _():
        need = k_row - count_rows(lambda blk: blk > thr)
        r_i = lax.broadcasted_iota(jnp.int32, (KC, KC), 0)
        c_i = lax.broadcasted_iota(jnp.int32, (KC, KC), 1)
        tri = jnp.where(c_i <= r_i, 1.0, 0.0).astype(jnp.bfloat16)

        def drop_body(c, seen):
            r0 = pl.multiple_of(c * KC, KC)
            blk = sc_ref[pl.ds(r0, KC), :]
            eq = blk == thr
            eq_f = jnp.where(eq, 1.0, 0.0)
            incl = jnp.dot(tri, eq_f.astype(jnp.bfloat16), preferred_element_type=jnp.float32)
            rank = seen + incl - eq_f
            sc_ref[pl.ds(r0, KC), :] = jnp.where(eq & (rank >= need), -jnp.inf, blk)
            return seen + jnp.sum(eq_f, axis=0, keepdims=True)

        lax.fori_loop(0, nch, drop_body, jnp.zeros((1, Q), jnp.float32))

    m_ref[...] = jnp.full(m_ref.shape, NEG_BIG, jnp.float32)
    acc_ref[...] = jnp.zeros(acc_ref.shape, jnp.float32)
    QH = Q_BLOCK
    n_half = Q // QH

    def att_chunk(c, carry):
        r0 = pl.multiple_of(c * KC, KC)
        ka = ka_ref[pl.ds(r0, KC), :]
        vt = vt_ref[c]
        mask_bias = jnp.where(sc_ref[pl.ds(r0, KC), :] >= thr, 0.0, NEG_BIG)
        for g in range(n_half):
            rows = slice(g * QH, (g + 1) * QH)
            mb = mask_bias[:, rows]
            for p in range(N_ATT_HEADS // 2):
                q_pair = jnp.concatenate(
                    [qa_ref[rows, (2 * p) * QK_AUG:(2 * p + 1) * QK_AUG],
                     qa_ref[rows, (2 * p + 1) * QK_AUG:(2 * p + 2) * QK_AUG]], axis=0)
                lg = lax.dot_general(ka, q_pair, nt, preferred_element_type=jnp.float32)
                ps, alphas = [], []
                for hh in range(2):
                    h = 2 * p + hh
                    lgh = lg[:, hh * QH:(hh + 1) * QH] + mb
                    m_old = m_ref[g, h][0:1, :]
                    m_new = jnp.maximum(m_old, jnp.max(lgh, axis=0, keepdims=True))
                    alphas.append(jnp.exp(m_old - m_new))
                    ps.append(jnp.exp(lgh - m_new).astype(jnp.bfloat16))
                    m_ref[g, h] = jnp.broadcast_to(m_new, (8, QH))
                pv = jnp.dot(vt, jnp.concatenate(ps, axis=1), preferred_element_type=jnp.float32)
                acc_ref[g, p] = acc_ref[g, p] * jnp.concatenate(alphas, axis=1) + pv
        return carry

    lax.fori_loop(0, nch, att_chunk, 0)

    for g in range(n_half):
        for h in range(N_ATT_HEADS):
            a = acc_ref[g, h // 2][:, (h % 2) * QH:(h % 2 + 1) * QH]
            out_t = a[0:ATT_HEAD_DIM, :] / a[ATT_HEAD_DIM:ATT_HEAD_DIM + 1, :]
            o_ref[g * QH:(g + 1) * QH, h * ATT_HEAD_DIM:(h + 1) * ATT_HEAD_DIM] = out_t.T.astype(o_ref.dtype)


def _attn(qa, proj, widx, kd, ka, vt, B, S, n_sel):
    T = B * S
    nb = S // ATTN_Q
    n_kc = S // KEY_CHUNK
    qrow = lambda b, j: (b * nb + j, 0)
    per_b = lambda b, j: (b, 0)
    return pl.pallas_call(
        functools.partial(_attn_kernel, n_sel),
        out_shape=jax.ShapeDtypeStruct((T, D_MODEL), jnp.bfloat16),
        grid=(B, nb),
        in_specs=[pl.BlockSpec((ATTN_Q, N_ATT_HEADS * QK_AUG), qrow),
                  pl.BlockSpec((ATTN_Q, 512), lambda b, j: (b * nb + j, C_QIDX // 512)),
                  pl.BlockSpec((ATTN_Q, LANES), qrow),
                  pl.BlockSpec((S, LANES), per_b),
                  pl.BlockSpec((S, QK_AUG), per_b),
                  pl.BlockSpec((n_kc, VT_ROWS, KEY_CHUNK), lambda b, j: (b, 0, 0))],
        out_specs=pl.BlockSpec((ATTN_Q, D_MODEL), qrow),
        scratch_shapes=[pltpu.VMEM((S, ATTN_Q), jnp.float32),
                        pltpu.VMEM((8, ATTN_Q), jnp.int32),
                        pltpu.VMEM((8, ATTN_Q), jnp.float32),
                        pltpu.VMEM((IDX_HEADS // 2, 2 * ATTN_Q, LANES), jnp.bfloat16),
                        pltpu.VMEM((ATTN_Q // Q_BLOCK, N_ATT_HEADS, 8, Q_BLOCK), jnp.float32),
                        pltpu.VMEM((ATTN_Q // Q_BLOCK, N_ATT_HEADS // 2, VT_ROWS, 2 * Q_BLOCK), jnp.float32)],
        compiler_params=_cparams(("parallel", "arbitrary")),
        name="attn",
    )(qa, proj, widx, kd, ka, vt)


def _ret_kernel(rq_ref, rk_ref, rv_ref, gate_ref, g_ref, o_ref, state_ref):
    C = RET_CHUNK

    @pl.when(pl.program_id(1) == 0)
    def _():
        state_ref[...] = jnp.zeros(state_ref.shape, jnp.float32)

    ii = lax.broadcasted_iota(jnp.int32, (C, C), 0)
    jj = lax.broadcasted_iota(jnp.int32, (C, C), 1)
    rel = (ii - jj).astype(jnp.float32)
    pos = lax.broadcasted_iota(jnp.int32, (C, 1), 0).astype(jnp.float32)
    for h in range(N_RET_HEADS):
        log_g = math.log(1.0 - 2.0 ** (-5.0 - h))
        decay_intra = jnp.where(rel >= 0, jnp.exp(rel * log_g), 0.0)
        q_decay = jnp.exp((pos + 1.0) * log_g)
        k_decay = jnp.exp((C - 1.0 - pos) * log_g)
        chunk_decay = math.exp(C * log_g)
        sl = slice(h * RET_V_DIM, (h + 1) * RET_V_DIM)
        for b in range(rq_ref.shape[0]):
            q = rq_ref[b, :, h * RET_QK_DIM:(h + 1) * RET_QK_DIM].astype(jnp.float32)
            k = rk_ref[b, :, h * RET_QK_DIM:(h + 1) * RET_QK_DIM].astype(jnp.float32) * (RET_QK_DIM ** -0.5)
            v = rv_ref[b, :, sl]
            state = state_ref[b, h]

            qk = lax.dot_general(q.astype(jnp.bfloat16), k.astype(jnp.bfloat16),
                                 (((1,), (1,)), ((), ())), preferred_element_type=jnp.float32)
            intra = (qk * decay_intra).astype(jnp.bfloat16)
            o = (jnp.dot(intra, v, preferred_element_type=jnp.float32)
                 + jnp.dot((q * q_decay).astype(jnp.bfloat16), state.astype(jnp.bfloat16),
                           preferred_element_type=jnp.float32))
            kd_t = (k * k_decay).T.astype(jnp.bfloat16)
            state_ref[b, h] = chunk_decay * state + jnp.dot(kd_t, v, preferred_element_type=jnp.float32)

            mu = jnp.mean(o, axis=-1, keepdims=True)
            var = jnp.mean(jnp.square(o - mu), axis=-1, keepdims=True)
            y = (o - mu) * lax.rsqrt(var + EPS) * g_ref[:, sl]
            gate = gate_ref[b, :, sl].astype(jnp.float32)
            o_ref[b, :, sl] = (gate * jax.nn.sigmoid(gate) * y).astype(o_ref.dtype)


def _ret(proj, ret_g, B, S):
    n = S // RET_CHUNK
    C = RET_CHUNK
    nbt = RET_BATCHES if B % RET_BATCHES == 0 else 1
    proj3 = proj.reshape(B, S, PROJ_W)
    out = pl.pallas_call(
        _ret_kernel,
        out_shape=jax.ShapeDtypeStruct((B, S, N_RET_HEADS * RET_V_DIM), jnp.bfloat16),
        grid=(B // nbt, n),
        in_specs=[pl.BlockSpec((nbt, C, 512), lambda b, i: (b, i, C_RQ // 512)),
                  pl.BlockSpec((nbt, C, 512), lambda b, i: (b, i, C_RK // 512)),
                  pl.BlockSpec((nbt, C, 1024), lambda b, i: (b, i, C_RV // 1024)),
                  pl.BlockSpec((nbt, C, 1024), lambda b, i: (b, i, C_RGATE // 1024)),
                  pl.BlockSpec((1, 1024), lambda b, i: (0, 0))],
        out_specs=pl.BlockSpec((nbt, C, 1024), lambda b, i: (b, i, 0)),
        scratch_shapes=[pltpu.VMEM((nbt, N_RET_HEADS, RET_QK_DIM, RET_V_DIM), jnp.float32)],
        compiler_params=_cparams(("parallel", "arbitrary")),
        name="ret",
    )(proj3, proj3, proj3, proj3, ret_g)
    return out.reshape(B * S, N_RET_HEADS * RET_V_DIM)


def _merge_kernel(att_ref, ret_ref, ga_ref, gr_ref, x_ref, wa_ref, wr_ref, wo_ref,
                  g2_ref, wrt_ref, brt_ref, h_ref, n2_ref, comb_ref):
    y_att = jnp.dot(att_ref[...], wa_ref[...], preferred_element_type=jnp.float32)
    y_ret = jnp.dot(ret_ref[...], wr_ref[...], preferred_element_type=jnp.float32)
    mixed = (jax.nn.sigmoid(ga_ref[...].astype(jnp.float32)) * y_att
             + jax.nn.sigmoid(gr_ref[...].astype(jnp.float32)) * y_ret)
    h = x_ref[...] + jnp.dot(mixed.astype(jnp.bfloat16), wo_ref[...],
                             preferred_element_type=jnp.float32)
    h_ref[...] = h
    ms = jnp.mean(h * h, axis=-1, keepdims=True)
    n2 = (h * lax.rsqrt(ms + EPS) * g2_ref[...]).astype(jnp.bfloat16)
    n2_ref[...] = n2

    logits = jnp.dot(n2, wrt_ref[...], preferred_element_type=jnp.float32) + brt_ref[...]
    lane = lax.broadcasted_iota(jnp.int32, logits.shape, 1)
    big = jnp.int32(ROUTER_W)
    is_g = lane < N_GROUPS
    g_max = jnp.max(jnp.where(is_g, logits, -jnp.inf), axis=-1, keepdims=True)
    g_sel = jnp.min(jnp.where(is_g & (logits == g_max), lane, big), axis=-1, keepdims=True)
    g_w = 1.0 / jnp.sum(jnp.where(is_g, jnp.exp(logits - g_max), 0.0), axis=-1, keepdims=True)
    lo = N_GROUPS + g_sel * EXPERTS_PER_GROUP
    in_grp = (lane >= lo) & (lane < lo + EXPERTS_PER_GROUP)
    v1 = jnp.max(jnp.where(in_grp, logits, -jnp.inf), axis=-1, keepdims=True)
    i1 = jnp.min(jnp.where(in_grp & (logits == v1), lane, big), axis=-1, keepdims=True)
    rest = in_grp & (lane != i1)
    v2 = jnp.max(jnp.where(rest, logits, -jnp.inf), axis=-1, keepdims=True)
    i2 = jnp.min(jnp.where(rest & (logits == v2), lane, big), axis=-1, keepdims=True)
    e2 = jnp.exp(v2 - v1)
    p1 = g_w / (1.0 + e2)
    p2 = p1 * e2
    comb_ref[...] = jnp.where(lane == i1, p1, 0.0) + jnp.where(lane == i2, p2, 0.0)


def _merge(att, gret, proj, x2, wa, wr, wo, g2, w_router, b_router, tm):
    T = x2.shape[0]
    row = lambda i: (i, 0)
    const = lambda i: (0, 0)
    return pl.pallas_call(
        _merge_kernel,
        out_shape=(jax.ShapeDtypeStruct((T, D_MODEL), jnp.float32),
                   jax.ShapeDtypeStruct((T, D_MODEL), jnp.bfloat16),
                   jax.ShapeDtypeStruct((T, ROUTER_W), jnp.float32)),
        grid=(T // tm,),
        in_specs=[pl.BlockSpec((tm, 1024), row),
                  pl.BlockSpec((tm, 1024), row),
                  pl.BlockSpec((tm, 1024), lambda i: (i, C_GATT // 1024)),
                  pl.BlockSpec((tm, 1024), lambda i: (i, C_GRET // 1024)),
                  pl.BlockSpec((tm, 1024), row),
                  pl.BlockSpec((1024, 1024), const),
                  pl.BlockSpec((1024, 1024), const),
                  pl.BlockSpec((1024, 1024), const),
                  pl.BlockSpec((1, 1024), const),
                  pl.BlockSpec((1024, ROUTER_W), const),
                  pl.BlockSpec((1, ROUTER_W), const)],
        out_specs=(pl.BlockSpec((tm, 1024), row),
                   pl.BlockSpec((tm, 1024), row),
                   pl.BlockSpec((tm, ROUTER_W), row)),
        compiler_params=_cparams(("parallel",)),
        name="merge",
    )(att, gret, proj, proj, x2, wa, wr, wo, g2, w_router, b_router)


def _moe_kernel(n2_ref, comb_ref, h_ref, wg_ref, wu_ref, wd_ref, o_ref, acc_ref):
    g = pl.program_id(1)
    n2 = n2_ref[...]
    comb = comb_ref[...]
    lane = lax.broadcasted_iota(jnp.int32, comb.shape, 1)
    acts = []
    for e in range(EXPERTS_PER_GROUP):
        c = jnp.sum(jnp.where(lane == N_GROUPS + g * EXPERTS_PER_GROUP + e, comb, 0.0),
                    axis=-1, keepdims=True)
        hg = jnp.dot(n2, wg_ref[e], preferred_element_type=jnp.float32)
        hu = jnp.dot(n2, wu_ref[e], preferred_element_type=jnp.float32)
        acts.append((hg * jax.nn.sigmoid(hg) * hu * c).astype(jnp.bfloat16))
    y = jnp.dot(jnp.concatenate(acts, axis=1), wd_ref[0], preferred_element_type=jnp.float32)

    @pl.when(g == 0)
    def _():
        acc_ref[...] = y

    @pl.when(g > 0)
    def _():
        acc_ref[...] += y

    @pl.when(g == N_GROUPS - 1)
    def _():
        o_ref[...] = h_ref[...] + acc_ref[...]


def _moe(n2, comb, h1, w_g, w_u, w_d, tm):
    T = n2.shape[0]
    E, F = EXPERTS_PER_GROUP, EXPERT_FF
    row = lambda i, g: (i, 0)
    grp = lambda i, g: (g, 0, 0)
    return pl.pallas_call(
        _moe_kernel,
        out_shape=jax.ShapeDtypeStruct((T, D_MODEL), jnp.float32),
        grid=(T // tm, N_GROUPS),
        in_specs=[pl.BlockSpec((tm, D_MODEL), row),
                  pl.BlockSpec((tm, ROUTER_W), row),
                  pl.BlockSpec((tm, D_MODEL), row),
                  pl.BlockSpec((E, D_MODEL, F), grp),
                  pl.BlockSpec((E, D_MODEL, F), grp),
                  pl.BlockSpec((1, E * F, D_MODEL), grp)],
        out_specs=pl.BlockSpec((tm, D_MODEL), row),
        scratch_shapes=[pltpu.VMEM((tm, D_MODEL), jnp.float32)],
        compiler_params=_cparams(("parallel", "arbitrary")),
        name="moe",
    )(n2, comb, h1, w_g, w_u, w_d)


def _relayout_w_in(w):
    pad = jnp.zeros((w.shape[0], C_RQ - (C_WIDX + IDX_HEADS)), w.dtype)
    return jnp.concatenate(
        [w[:, 0:1024],
         w[:, 1280:1792],
         w[:, 1024:1280],
         w[:, 1792:1864],
         pad,
         w[:, 1864:]], axis=1)


def _layer(h, norm1_g, w_in, kv_norm_g, w_kv_up, q_norm_g, k_norm_g, idx_k_norm_g,
           ret_norm_g, w_att_branch, w_ret_branch, w_out, norm2_g, w_group_router,
           b_group_router, w_expert_router, b_expert_router, w_exp_gate, w_exp_up, w_exp_down):
    B, S, D = h.shape
    assert D == D_MODEL and S % KEY_CHUNK == 0 and S % Q_BLOCK == 0 and S <= 65536
    T = B * S
    n_sel = min(TOPK_MAX, S // 4)
    tm = 512 if S % 512 == 0 else 256
    bf = jnp.bfloat16
    f32 = jnp.float32

    x2 = h.reshape(T, D)
    w_p = _relayout_w_in(w_in).astype(bf)
    proj = _proj(x2, norm1_g.reshape(1, D).astype(f32), w_p, tm, 1024)

    idx_g_pad = jnp.concatenate([idx_k_norm_g.astype(f32), jnp.zeros((LANES - IDX_DIM,), f32)])
    qa, ka, vt, kd, widx = _prep(
        proj, q_norm_g.reshape(1, -1).astype(f32), kv_norm_g.reshape(1, -1).astype(f32),
        w_kv_up.astype(bf), k_norm_g.reshape(1, -1).astype(f32), idx_g_pad.reshape(1, LANES), tm, S)

    att = _attn(qa, proj, widx, kd, ka, vt, B, S, n_sel)
    gret = _ret(proj, ret_norm_g.reshape(1, -1).astype(f32), B, S)

    w_router = jnp.concatenate(
        [w_group_router, w_expert_router,
         jnp.zeros((D, ROUTER_W - N_GROUPS - N_EXPERTS), w_group_router.dtype)], axis=1).astype(bf)
    b_router = jnp.concatenate(
        [b_group_router, b_expert_router,
         jnp.zeros((ROUTER_W - N_GROUPS - N_EXPERTS,), b_group_router.dtype)]).reshape(1, ROUTER_W)
    h1, n2, comb = _merge(att, gret, proj, x2, w_att_branch.astype(bf), w_ret_branch.astype(bf),
                          w_out.astype(bf), norm2_g.reshape(1, D).astype(f32),
                          w_router, b_router.astype(f32), tm)

    w_d = w_exp_down.astype(bf).reshape(N_GROUPS, EXPERTS_PER_GROUP * EXPERT_FF, D)
    out = _moe(n2, comb, h1, w_exp_gate.astype(bf), w_exp_up.astype(bf), w_d, 1024 if T % 1024 == 0 else tm)
    return out.reshape(B, S, D)


def kernel(x, norm1_g, w_in, kv_norm_g, w_kv_up, q_norm_g, k_norm_g, idx_k_norm_g, ret_norm_g,
           w_att_branch, w_ret_branch, w_out, norm2_g, w_group_router, b_group_router,
           w_expert_router, b_expert_router, w_exp_gate, w_exp_up, w_exp_down):
    h = x
    for l in range(norm1_g.shape[0]):
        h = _layer(h, norm1_g[l], w_in[l], kv_norm_g[l], w_kv_up[l], q_norm_g[l], k_norm_g[l],
                   idx_k_norm_g[l], ret_norm_g[l], w_att_branch[l], w_ret_branch[l], w_out[l],
                   norm2_g[l], w_group_router[l], b_group_router[l], w_expert_router[l],
                   b_expert_router[l], w_exp_gate[l], w_exp_up[l], w_exp_down[l])
    return h
```

```python
import functools
import math

import jax
import jax.numpy as jnp
from jax import lax
from jax.experimental import pallas as pl
from jax.experimental.pallas import tpu as pltpu

D_MODEL = 1024
N_ATT_HEADS = 8
ATT_HEAD_DIM = 128
KV_LORA = 256
IDX_HEADS = 8
IDX_DIM = 64
TOPK_MAX = 256
Q_BLOCK = 128
ATTN_Q = 256
N_RET_HEADS = 4
RET_QK_DIM = 128
RET_V_DIM = 256
RET_CHUNK = 128
N_GROUPS = 4
EXPERTS_PER_GROUP = 4
N_EXPERTS = N_GROUPS * EXPERTS_PER_GROUP
EXPERT_FF = 256
EPS = 1e-6

LANES = 128
KEY_CHUNK = 256
QK_AUG = 256
RET_BATCHES = 4
N_SWEEP_ACC = 4
VT_ROWS = 144
PROJ_W = 7168
C_QATT, C_QIDX, C_CKV, C_KIDX, C_WIDX = 0, 1024, 1536, 1792, 1856
C_RQ, C_RK, C_RV, C_RGATE, C_GATT, C_GRET = 2048, 2560, 3072, 4096, 5120, 6144
ROUTER_W = 128
NEG_BIG = -1e30
VMEM_LIMIT = 56 * 1024 * 1024

_INT_MIN = -(2 ** 31)


def _cparams(sem):
    return pltpu.CompilerParams(dimension_semantics=sem, vmem_limit_bytes=VMEM_LIMIT)


def _proj_kernel(tn, x_ref, g_ref, w_ref, o_ref):
    x = x_ref[...]
    ms = jnp.mean(x * x, axis=-1, keepdims=True)
    n = (x * lax.rsqrt(ms + EPS) * g_ref[...]).astype(jnp.bfloat16)
    for c in range(PROJ_W // tn):
        o_ref[:, c * tn:(c + 1) * tn] = jnp.dot(
            n, w_ref[:, c * tn:(c + 1) * tn], preferred_element_type=jnp.float32).astype(o_ref.dtype)


def _proj(x2, g1, w_p, tm, tn):
    T = x2.shape[0]
    return pl.pallas_call(
        functools.partial(_proj_kernel, tn),
        out_shape=jax.ShapeDtypeStruct((T, PROJ_W), jnp.bfloat16),
        grid=(T // tm,),
        in_specs=[pl.BlockSpec((tm, D_MODEL), lambda i: (i, 0)),
                  pl.BlockSpec((1, D_MODEL), lambda i: (0, 0)),
                  pl.BlockSpec((D_MODEL, PROJ_W), lambda i: (0, 0))],
        out_specs=pl.BlockSpec((tm, PROJ_W), lambda i: (i, 0)),
        compiler_params=_cparams(("parallel",)),
        name="proj",
    )(x2, g1, w_p)


def _prep_kernel(S, q_ref, c_ref, qg_ref, kvg_ref, wkv_ref, kg_ref, ig_ref,
                 qa_ref, ka_ref, vt_ref, kd_ref, w_ref):
    tm = q_ref.shape[0]
    lane = lax.broadcasted_iota(jnp.int32, (tm, LANES), 1)

    qg = qg_ref[...]
    for h in range(N_ATT_HEADS):
        qh = q_ref[:, h * ATT_HEAD_DIM:(h + 1) * ATT_HEAD_DIM].astype(jnp.float32)
        ms = jnp.mean(qh * qh, axis=-1, keepdims=True)
        qa_ref[:, h * ATT_HEAD_DIM:(h + 1) * ATT_HEAD_DIM] = (
            qh * lax.rsqrt(ms + EPS) * qg * (ATT_HEAD_DIM ** -0.5)).astype(jnp.bfloat16)

    c = c_ref[:, 0:KV_LORA].astype(jnp.float32)
    ms = jnp.mean(c * c, axis=-1, keepdims=True)
    cn = (c * lax.rsqrt(ms + EPS) * kvg_ref[...]).astype(jnp.bfloat16)
    kv = jnp.dot(cn, wkv_ref[...], preferred_element_type=jnp.float32)
    k = kv[:, :ATT_HEAD_DIM]
    ms = jnp.mean(k * k, axis=-1, keepdims=True)
    ka_ref[:, 0:ATT_HEAD_DIM] = (k * lax.rsqrt(ms + EPS) * kg_ref[...]).astype(jnp.bfloat16)
    base = (pl.program_id(0) % (S // tm)) * tm
    pos = base + lax.broadcasted_iota(jnp.int32, (tm, LANES), 0)
    pos_lo = pos & 255
    ka_ref[:, ATT_HEAD_DIM:] = jnp.where(
        lane == 0, pos - pos_lo, jnp.where(lane == 1, pos_lo, 0)).astype(jnp.float32).astype(jnp.bfloat16)
    v = kv[:, ATT_HEAD_DIM:]
    ones_row = jnp.where(lax.broadcasted_iota(jnp.int32, (VT_ROWS - ATT_HEAD_DIM, KEY_CHUNK), 0) == 0,
                         1.0, 0.0).astype(jnp.bfloat16)
    for u in range(tm // KEY_CHUNK):
        vt_ref[u, 0:ATT_HEAD_DIM, :] = v[u * KEY_CHUNK:(u + 1) * KEY_CHUNK, :].T.astype(jnp.bfloat16)
        vt_ref[u, ATT_HEAD_DIM:, :] = ones_row

    blk = c_ref[:, C_KIDX - C_CKV:C_KIDX - C_CKV + LANES].astype(jnp.float32)
    is_k = lane < IDX_DIM
    ms = jnp.sum(jnp.where(is_k, blk * blk, 0.0), axis=-1, keepdims=True) * (1.0 / IDX_DIM)
    kn = jnp.where(is_k, blk * lax.rsqrt(ms + EPS) * ig_ref[...], 0.0)
    kd_ref[...] = (kn + pltpu.roll(kn, IDX_DIM, 1)).astype(jnp.bfloat16)
    w_ref[...] = blk * ((IDX_HEADS ** -0.5) * (IDX_DIM ** -0.5))


def _prep(proj, q_norm_g, kv_norm_g, w_kv_up, k_norm_g, idx_g_pad, tm, S):
    T = proj.shape[0]
    row = lambda i: (i, 0)
    const = lambda i: (0, 0)
    return pl.pallas_call(
        functools.partial(_prep_kernel, S),
        out_shape=(jax.ShapeDtypeStruct((T, D_MODEL), jnp.bfloat16),
                   jax.ShapeDtypeStruct((T, QK_AUG), jnp.bfloat16),
                   jax.ShapeDtypeStruct((T // KEY_CHUNK, VT_ROWS, KEY_CHUNK), jnp.bfloat16),
                   jax.ShapeDtypeStruct((T, LANES), jnp.bfloat16),
                   jax.ShapeDtypeStruct((T, LANES), jnp.float32)),
        grid=(T // tm,),
        in_specs=[pl.BlockSpec((tm, 1024), lambda i: (i, C_QATT // 1024)),
                  pl.BlockSpec((tm, 512), lambda i: (i, C_CKV // 512)),
                  pl.BlockSpec((1, ATT_HEAD_DIM), const),
                  pl.BlockSpec((1, KV_LORA), const),
                  pl.BlockSpec((KV_LORA, 2 * ATT_HEAD_DIM), const),
                  pl.BlockSpec((1, ATT_HEAD_DIM), const),
                  pl.BlockSpec((1, LANES), const)],
        out_specs=(pl.BlockSpec((tm, D_MODEL), row),
                   pl.BlockSpec((tm, QK_AUG), row),
                   pl.BlockSpec((tm // KEY_CHUNK, VT_ROWS, KEY_CHUNK), lambda i: (i, 0, 0)),
                   pl.BlockSpec((tm, LANES), row),
                   pl.BlockSpec((tm, LANES), row)),
        compiler_params=_cparams(("parallel",)),
        name="prep",
    )(proj, proj, q_norm_g, kv_norm_g, w_kv_up, k_norm_g, idx_g_pad)


def _attn_kernel(n_sel, qa_ref, qi_ref, w_ref, kd_ref, ka_ref, vt_ref, o_ref,
                 sc_ref, pre_ref, cnt_ref, qs_ref, m_ref, acc_ref):
    j = pl.program_id(1)
    Q, KC = ATTN_Q, KEY_CHUNK
    nch = (j * Q + Q + KC - 1) // KC
    q_pos = j * Q + lax.broadcasted_iota(jnp.int32, (1, Q), 1)
    row_in_chunk = lax.broadcasted_iota(jnp.int32, (KC, Q), 0)
    nt = (((1,), (1,)), ((), ()))

    lane = lax.broadcasted_iota(jnp.int32, (Q, LANES), 1)
    for p in range(IDX_HEADS // 2):
        qp = qi_ref[:, p * LANES:(p + 1) * LANES].astype(jnp.float32)
        qs_ref[p, 0:Q, :] = jnp.where(lane < IDX_DIM, qp, 0.0).astype(jnp.bfloat16)
        qs_ref[p, Q:2 * Q, :] = jnp.where(lane >= IDX_DIM, qp, 0.0).astype(jnp.bfloat16)
    w_t = w_ref[...].T

    def score_chunk(c, carry):
        r0 = pl.multiple_of(c * KC, KC)
        kd = kd_ref[pl.ds(r0, KC), :]
        s = jnp.zeros((KC, Q), jnp.float32)
        for p in range(IDX_HEADS // 2):
            d = lax.dot_general(kd, qs_ref[p], nt, preferred_element_type=jnp.float32)
            h0 = IDX_DIM + 2 * p
            s = (s + w_t[h0:h0 + 1, :] * jnp.maximum(d[:, :Q], 0.0)
                 + w_t[h0 + 1:h0 + 2, :] * jnp.maximum(d[:, Q:], 0.0))
        sc_ref[pl.ds(r0, KC), :] = jnp.where(r0 + row_in_chunk <= q_pos, s, -jnp.inf)
        return carry

    lax.fori_loop(0, nch, score_chunk, 0)

    k_row = jnp.minimum(q_pos + 1, n_sel).astype(jnp.float32)

    def tree_sum(parts):
        while len(parts) > 1:
            odd = parts[-1:] if len(parts) % 2 else []
            parts = [a + b for a, b in zip(parts[0::2], parts[1::2])] + odd
        return parts[0]

    def count_rows(pred):
        def body(c, acc):
            r0 = pl.multiple_of(c * KC, KC)
            hit = jnp.where(pred(sc_ref[pl.ds(r0, KC), :]), 1.0, 0.0)
            return acc + tree_sum([hit[u * 8:(u + 1) * 8, :] for u in range(KC // 8)])

        acc = lax.fori_loop(0, nch, body, jnp.zeros((8, Q), jnp.float32))
        return jnp.sum(acc, axis=0, keepdims=True)

    def key_to_float(u):
        key = u ^ _INT_MIN
        bits = jnp.where(key >= 0, key, key ^ 0x7FFFFFFF)
        return lax.bitcast_convert_type(bits, jnp.float32)

    def bisect(n):
        def count_ge(cand):
            cand_b = jnp.broadcast_to(cand, (8, Q))
            accs = [jnp.zeros((8, Q), jnp.float32) for _ in range(N_SWEEP_ACC)]
            for u, r in enumerate(range(0, n * KC, 8)):
                accs[u % N_SWEEP_ACC] = accs[u % N_SWEEP_ACC] + jnp.where(sc_ref[r:r + 8, :] >= cand_b, 1.0, 0.0)
            return jnp.sum(tree_sum(accs), axis=0, keepdims=True)

        def bit_body(i, carry):
            prefix, cnt_best = carry
            cand_u = prefix | lax.shift_left(jnp.int32(1), 31 - i)
            cnt = count_ge(key_to_float(cand_u))
            take = cnt >= k_row
            return jnp.where(take, cand_u, prefix), jnp.where(take, cnt, cnt_best)

        prefix, cnt_best = lax.fori_loop(0, 32, bit_body, (jnp.zeros((1, Q), jnp.int32), k_row))
        pre_ref[...] = jnp.broadcast_to(prefix, (8, Q))
        cnt_ref[...] = jnp.broadcast_to(cnt_best, (8, Q))

    for n in range(1, sc_ref.shape[0] // KC + 1):
        pl.when(nch == n)(functools.partial(bisect, n))
    prefix = pre_ref[0:1, :]
    cnt_best = cnt_ref[0:1, :]
    thr = key_to_float(prefix)

    @pl.when(jnp.max(cnt_best - k_row) > 0.0)
    def _():
        need = k_row - count_rows(lambda blk: blk > thr)
        r_i = lax.broadcasted_iota(jnp.int32, (KC, KC), 0)
        c_i = lax.broadcasted_iota(jnp.int32, (KC, KC), 1)
        tri = jnp.where(c_i <= r_i, 1.0, 0.0).astype(jnp.bfloat16)

        def drop_body(c, seen):
            r0 = pl.multiple_of(c * KC, KC)
            blk = sc_ref[pl.ds(r0, KC), :]
            eq = blk == thr
            eq_f = jnp.where(eq, 1.0, 0.0)
            incl = jnp.dot(tri, eq_f.astype(jnp.bfloat16), preferred_element_type=jnp.float32)
            rank = seen + incl - eq_f
            sc_ref[pl.ds(r0, KC), :] = jnp.where(eq & (rank >= need), -jnp.inf, blk)
            return seen + jnp.sum(eq_f, axis=0, keepdims=True)

        lax.fori_loop(0, nch, drop_body, jnp.zeros((1, Q), jnp.float32))

    m_ref[...] = jnp.full(m_ref.shape, NEG_BIG, jnp.float32)
    acc_ref[...] = jnp.zeros(acc_ref.shape, jnp.float32)
    QH = Q_BLOCK
    n_half = Q // QH
    lane_h = lax.broadcasted_iota(jnp.int32, (QH, LANES), 1)
    slope_lanes = [jnp.where(lane_h < 2, 2.0 ** (-8.0 * (h + 1) / N_ATT_HEADS), 0.0).astype(jnp.bfloat16)
                   for h in range(N_ATT_HEADS)]

    def att_chunk(c, carry):
        r0 = pl.multiple_of(c * KC, KC)
        ka = ka_ref[pl.ds(r0, KC), :]
        vt = vt_ref[c]
        mask_bias = jnp.where(sc_ref[pl.ds(r0, KC), :] >= thr, 0.0, NEG_BIG)
        for g in range(n_half):
            rows = slice(g * QH, (g + 1) * QH)
            mb = mask_bias[:, rows]
            for p in range(N_ATT_HEADS // 2):
                q_pair = jnp.concatenate(
                    [jnp.concatenate([qa_ref[rows, h * ATT_HEAD_DIM:(h + 1) * ATT_HEAD_DIM], slope_lanes[h]], axis=1)
                     for h in (2 * p, 2 * p + 1)], axis=0)
                lg = lax.dot_general(ka, q_pair, nt, preferred_element_type=jnp.float32)
                ps, alphas = [], []
                for hh in range(2):
                    h = 2 * p + hh
                    lgh = lg[:, hh * QH:(hh + 1) * QH] + mb
                    m_old = m_ref[g, h][0:1, :]
                    m_new = jnp.maximum(m_old, jnp.max(lgh, axis=0, keepdims=True))
                    alphas.append(jnp.exp(m_old - m_new))
                    ps.append(jnp.exp(lgh - m_new).astype(jnp.bfloat16))
                    m_ref[g, h] = jnp.broadcast_to(m_new, (8, QH))
                pv = jnp.dot(vt, jnp.concatenate(ps, axis=1), preferred_element_type=jnp.float32)
                acc_ref[g, p] = acc_ref[g, p] * jnp.concatenate(alphas, axis=1) + pv
        return carry

    lax.fori_loop(0, nch, att_chunk, 0)

    for g in range(n_half):
        for h in range(N_ATT_HEADS):
            a = acc_ref[g, h // 2][:, (h % 2) * QH:(h % 2 + 1) * QH]
            out_t = a[0:ATT_HEAD_DIM, :] / a[ATT_HEAD_DIM:ATT_HEAD_DIM + 1, :]
            o_ref[g * QH:(g + 1) * QH, h * ATT_HEAD_DIM:(h + 1) * ATT_HEAD_DIM] = out_t.T.astype(o_ref.dtype)


def _attn(qa, proj, widx, kd, ka, vt, B, S, n_sel):
    T = B * S
    nb = S // ATTN_Q
    n_kc = S // KEY_CHUNK
    qrow = lambda b, j: (b * nb + j, 0)
    per_b = lambda b, j: (b, 0)
    return pl.pallas_call(
        functools.partial(_attn_kernel, n_sel),
        out_shape=jax.ShapeDtypeStruct((T, D_MODEL), jnp.bfloat16),
        grid=(B, nb),
        in_specs=[pl.BlockSpec((ATTN_Q, D_MODEL), qrow),
                  pl.BlockSpec((ATTN_Q, 512), lambda b, j: (b * nb + j, C_QIDX // 512)),
                  pl.BlockSpec((ATTN_Q, LANES), qrow),
                  pl.BlockSpec((S, LANES), per_b),
                  pl.BlockSpec((S, QK_AUG), per_b),
                  pl.BlockSpec((n_kc, VT_ROWS, KEY_CHUNK), lambda b, j: (b, 0, 0))],
        out_specs=pl.BlockSpec((ATTN_Q, D_MODEL), qrow),
        scratch_shapes=[pltpu.VMEM((S, ATTN_Q), jnp.float32),
                        pltpu.VMEM((8, ATTN_Q), jnp.int32),
                        pltpu.VMEM((8, ATTN_Q), jnp.float32),
                        pltpu.VMEM((IDX_HEADS // 2, 2 * ATTN_Q, LANES), jnp.bfloat16),
                        pltpu.VMEM((ATTN_Q // Q_BLOCK, N_ATT_HEADS, 8, Q_BLOCK), jnp.float32),
                        pltpu.VMEM((ATTN_Q // Q_BLOCK, N_ATT_HEADS // 2, VT_ROWS, 2 * Q_BLOCK), jnp.float32)],
        compiler_params=_cparams(("parallel", "arbitrary")),
        name="attn",
    )(qa, proj, widx, kd, ka, vt)


def _ret_kernel(rq_ref, rk_ref, rv_ref, gate_ref, g_ref, o_ref, state_ref):
    C = RET_CHUNK

    @pl.when(pl.program_id(1) == 0)
    def _():
        state_ref[...] = jnp.zeros(state_ref.shape, jnp.float32)

    ii = lax.broadcasted_iota(jnp.int32, (C, C), 0)
    jj = lax.broadcasted_iota(jnp.int32, (C, C), 1)
    rel = (ii - jj).astype(jnp.float32)
    pos = lax.broadcasted_iota(jnp.int32, (C, 1), 0).astype(jnp.float32)
    for h in range(N_RET_HEADS):
        log_g = math.log(1.0 - 2.0 ** (-5.0 - h))
        decay_intra = jnp.where(rel >= 0, jnp.exp(rel * log_g), 0.0)
        q_decay = jnp.exp((pos + 1.0) * log_g)
        k_decay = jnp.exp((C - 1.0 - pos) * log_g)
        chunk_decay = math.exp(C * log_g)
        sl = slice(h * RET_V_DIM, (h + 1) * RET_V_DIM)
        for b in range(rq_ref.shape[0]):
            q = rq_ref[b, :, h * RET_QK_DIM:(h + 1) * RET_QK_DIM].astype(jnp.float32)
            k = rk_ref[b, :, h * RET_QK_DIM:(h + 1) * RET_QK_DIM].astype(jnp.float32) * (RET_QK_DIM ** -0.5)
            v = rv_ref[b, :, sl]
            state = state_ref[b, h]

            qk = lax.dot_general(q.astype(jnp.bfloat16), k.astype(jnp.bfloat16),
                                 (((1,), (1,)), ((), ())), preferred_element_type=jnp.float32)
            intra = (qk * decay_intra).astype(jnp.bfloat16)
            o = (jnp.dot(intra, v, preferred_element_type=jnp.float32)
                 + jnp.dot((q * q_decay).astype(jnp.bfloat16), state.astype(jnp.bfloat16),
                           preferred_element_type=jnp.float32))
            kd_t = (k * k_decay).T.astype(jnp.bfloat16)
            state_ref[b, h] = chunk_decay * state + jnp.dot(kd_t, v, preferred_element_type=jnp.float32)

            mu = jnp.mean(o, axis=-1, keepdims=True)
            var = jnp.mean(jnp.square(o - mu), axis=-1, keepdims=True)
            y = (o - mu) * lax.rsqrt(var + EPS) * g_ref[:, sl]
            gate = gate_ref[b, :, sl].astype(jnp.float32)
            o_ref[b, :, sl] = (gate * jax.nn.sigmoid(gate) * y).astype(o_ref.dtype)


def _ret(proj, ret_g, B, S):
    n = S // RET_CHUNK
    C = RET_CHUNK
    nbt = RET_BATCHES if B % RET_BATCHES == 0 else 1
    proj3 = proj.reshape(B, S, PROJ_W)
    out = pl.pallas_call(
        _ret_kernel,
        out_shape=jax.ShapeDtypeStruct((B, S, N_RET_HEADS * RET_V_DIM), jnp.bfloat16),
        grid=(B // nbt, n),
        in_specs=[pl.BlockSpec((nbt, C, 512), lambda b, i: (b, i, C_RQ // 512)),
                  pl.BlockSpec((nbt, C, 512), lambda b, i: (b, i, C_RK // 512)),
                  pl.BlockSpec((nbt, C, 1024), lambda b, i: (b, i, C_RV // 1024)),
                  pl.BlockSpec((nbt, C, 1024), lambda b, i: (b, i, C_RGATE // 1024)),
                  pl.BlockSpec((1, 1024), lambda b, i: (0, 0))],
        out_specs=pl.BlockSpec((nbt, C, 1024), lambda b, i: (b, i, 0)),
        scratch_shapes=[pltpu.VMEM((nbt, N_RET_HEADS, RET_QK_DIM, RET_V_DIM), jnp.float32)],
        compiler_params=_cparams(("parallel", "arbitrary")),
        name="ret",
    )(proj3, proj3, proj3, proj3, ret_g)
    return out.reshape(B * S, N_RET_HEADS * RET_V_DIM)


def _merge_kernel(att_ref, ret_ref, ga_ref, gr_ref, x_ref, wa_ref, wr_ref, wo_ref,
                  g2_ref, wrt_ref, brt_ref, h_ref, n2_ref, comb_ref):
    y_att = jnp.dot(att_ref[...], wa_ref[...], preferred_element_type=jnp.float32)
    y_ret = jnp.dot(ret_ref[...], wr_ref[...], preferred_element_type=jnp.float32)
    mixed = (jax.nn.sigmoid(ga_ref[...].astype(jnp.float32)) * y_att
             + jax.nn.sigmoid(gr_ref[...].astype(jnp.float32)) * y_ret)
    h = x_ref[...] + jnp.dot(mixed.astype(jnp.bfloat16), wo_ref[...],
                             preferred_element_type=jnp.float32)
    h_ref[...] = h
    ms = jnp.mean(h * h, axis=-1, keepdims=True)
    n2 = (h * lax.rsqrt(ms + EPS) * g2_ref[...]).astype(jnp.bfloat16)
    n2_ref[...] = n2

    logits = jnp.dot(n2, wrt_ref[...], preferred_element_type=jnp.float32) + brt_ref[...]
    lane = lax.broadcasted_iota(jnp.int32, logits.shape, 1)
    big = jnp.int32(ROUTER_W)
    is_g = lane < N_GROUPS
    g_max = jnp.max(jnp.where(is_g, logits, -jnp.inf), axis=-1, keepdims=True)
    g_sel = jnp.min(jnp.where(is_g & (logits == g_max), lane, big), axis=-1, keepdims=True)
    g_w = 1.0 / jnp.sum(jnp.where(is_g, jnp.exp(logits - g_max), 0.0), axis=-1, keepdims=True)
    lo = N_GROUPS + g_sel * EXPERTS_PER_GROUP
    in_grp = (lane >= lo) & (lane < lo + EXPERTS_PER_GROUP)
    v1 = jnp.max(jnp.where(in_grp, logits, -jnp.inf), axis=-1, keepdims=True)
    i1 = jnp.min(jnp.where(in_grp & (logits == v1), lane, big), axis=-1, keepdims=True)
    rest = in_grp & (lane != i1)
    v2 = jnp.max(jnp.where(rest, logits, -jnp.inf), axis=-1, keepdims=True)
    i2 = jnp.min(jnp.where(rest & (logits == v2), lane, big), axis=-1, keepdims=True)
    e2 = jnp.exp(v2 - v1)
    p1 = g_w / (1.0 + e2)
    p2 = p1 * e2
    comb_ref[...] = jnp.where(lane == i1, p1, 0.0) + jnp.where(lane == i2, p2, 0.0)


def _merge(att, gret, proj, x2, wa, wr, wo, g2, w_router, b_router, tm):
    T = x2.shape[0]
    row = lambda i: (i, 0)
    const = lambda i: (0, 0)
    return pl.pallas_call(
        _merge_kernel,
        out_shape=(jax.ShapeDtypeStruct((T, D_MODEL), jnp.float32),
                   jax.ShapeDtypeStruct((T, D_MODEL), jnp.bfloat16),
                   jax.ShapeDtypeStruct((T, ROUTER_W), jnp.float32)),
        grid=(T // tm,),
        in_specs=[pl.BlockSpec((tm, 1024), row),
                  pl.BlockSpec((tm, 1024), row),
                  pl.BlockSpec((tm, 1024), lambda i: (i, C_GATT // 1024)),
                  pl.BlockSpec((tm, 1024), lambda i: (i, C_GRET // 1024)),
                  pl.BlockSpec((tm, 1024), row),
                  pl.BlockSpec((1024, 1024), const),
                  pl.BlockSpec((1024, 1024), const),
                  pl.BlockSpec((1024, 1024), const),
                  pl.BlockSpec((1, 1024), const),
                  pl.BlockSpec((1024, ROUTER_W), const),
                  pl.BlockSpec((1, ROUTER_W), const)],
        out_specs=(pl.BlockSpec((tm, 1024), row),
                   pl.BlockSpec((tm, 1024), row),
                   pl.BlockSpec((tm, ROUTER_W), row)),
        compiler_params=_cparams(("parallel",)),
        name="merge",
    )(att, gret, proj, proj, x2, wa, wr, wo, g2, w_router, b_router)


def _moe_kernel(n2_ref, comb_ref, h_ref, wg_ref, wu_ref, wd_ref, o_ref):
    n2 = n2_ref[...]
    comb = comb_ref[...]
    lane = lax.broadcasted_iota(jnp.int32, comb.shape, 1)
    y = h_ref[...]
    for g in range(N_GROUPS):
        acts = []
        for e in range(EXPERTS_PER_GROUP):
            ex = g * EXPERTS_PER_GROUP + e
            c = jnp.sum(jnp.where(lane == N_GROUPS + ex, comb, 0.0), axis=-1, keepdims=True)
            hg = jnp.dot(n2, wg_ref[ex], preferred_element_type=jnp.float32)
            hu = jnp.dot(n2, wu_ref[ex], preferred_element_type=jnp.float32)
            acts.append((hg * jax.nn.sigmoid(hg) * hu * c).astype(jnp.bfloat16))
        y = y + jnp.dot(jnp.concatenate(acts, axis=1), wd_ref[g], preferred_element_type=jnp.float32)
    o_ref[...] = y


def _moe(n2, comb, h1, w_g, w_u, w_d, tm):
    T = n2.shape[0]
    E, F = EXPERTS_PER_GROUP, EXPERT_FF
    row = lambda i: (i, 0)
    whole = lambda i: (0, 0, 0)
    resident = dict(pipeline_mode=pl.Buffered(1))
    return pl.pallas_call(
        _moe_kernel,
        out_shape=jax.ShapeDtypeStruct((T, D_MODEL), jnp.float32),
        grid=(T // tm,),
        in_specs=[pl.BlockSpec((tm, D_MODEL), row),
                  pl.BlockSpec((tm, ROUTER_W), row),
                  pl.BlockSpec((tm, D_MODEL), row),
                  pl.BlockSpec((N_EXPERTS, D_MODEL, F), whole, **resident),
                  pl.BlockSpec((N_EXPERTS, D_MODEL, F), whole, **resident),
                  pl.BlockSpec((N_GROUPS, E * F, D_MODEL), whole, **resident)],
        out_specs=pl.BlockSpec((tm, D_MODEL), row),
        compiler_params=_cparams(("parallel",)),
        name="moe",
    )(n2, comb, h1, w_g, w_u, w_d)


def _relayout_w_in(w):
    pad = jnp.zeros((w.shape[0], C_RQ - (C_WIDX + IDX_HEADS)), w.dtype)
    return jnp.concatenate(
        [w[:, 0:1024],
         w[:, 1280:1792],
         w[:, 1024:1280],
         w[:, 1792:1864],
         pad,
         w[:, 1864:]], axis=1)


def _layer(h, norm1_g, w_in, kv_norm_g, w_kv_up, q_norm_g, k_norm_g, idx_k_norm_g,
           ret_norm_g, w_att_branch, w_ret_branch, w_out, norm2_g, w_group_router,
           b_group_router, w_expert_router, b_expert_router, w_exp_gate, w_exp_up, w_exp_down):
    B, S, D = h.shape
    assert D == D_MODEL and S % KEY_CHUNK == 0 and S % Q_BLOCK == 0 and S <= 65536
    T = B * S
    n_sel = min(TOPK_MAX, S // 4)
    tm = 512 if S % 512 == 0 else 256
    bf = jnp.bfloat16
    f32 = jnp.float32

    x2 = h.reshape(T, D)
    w_p = _relayout_w_in(w_in.astype(bf))
    proj = _proj(x2, norm1_g.reshape(1, D).astype(f32), w_p, tm, 1024)

    idx_g_pad = jnp.concatenate([idx_k_norm_g.astype(f32), jnp.zeros((LANES - IDX_DIM,), f32)])
    qa, ka, vt, kd, widx = _prep(
        proj, q_norm_g.reshape(1, -1).astype(f32), kv_norm_g.reshape(1, -1).astype(f32),
        w_kv_up.astype(bf), k_norm_g.reshape(1, -1).astype(f32), idx_g_pad.reshape(1, LANES), tm, S)

    att = _attn(qa, proj, widx, kd, ka, vt, B, S, n_sel)
    gret = _ret(proj, ret_norm_g.reshape(1, -1).astype(f32), B, S)

    w_router = jnp.concatenate(
        [w_group_router, w_expert_router,
         jnp.zeros((D, ROUTER_W - N_GROUPS - N_EXPERTS), w_group_router.dtype)], axis=1).astype(bf)
    b_router = jnp.concatenate(
        [b_group_router, b_expert_router,
         jnp.zeros((ROUTER_W - N_GROUPS - N_EXPERTS,), b_group_router.dtype)]).reshape(1, ROUTER_W)
    h1, n2, comb = _merge(att, gret, proj, x2, w_att_branch.astype(bf), w_ret_branch.astype(bf),
                          w_out.astype(bf), norm2_g.reshape(1, D).astype(f32),
                          w_router, b_router.astype(f32), tm)

    w_d = w_exp_down.astype(bf).reshape(N_GROUPS, EXPERTS_PER_GROUP * EXPERT_FF, D)
    out = _moe(n2, comb, h1, w_exp_gate.astype(bf), w_exp_up.astype(bf), w_d, tm)
    return out.reshape(B, S, D)


def kernel(x, norm1_g, w_in, kv_norm_g, w_kv_up, q_norm_g, k_norm_g, idx_k_norm_g, ret_norm_g,
           w_att_branch, w_ret_branch, w_out, norm2_g, w_group_router, b_group_router,
           w_expert_router, b_expert_router, w_exp_gate, w_exp_up, w_exp_down):
    h = x
    for l in range(norm1_g.shape[0]):
        h = _layer(h, norm1_g[l], w_in[l], kv_norm_g[l], w_kv_up[l], q_norm_g[l], k_norm_g[l],
                   idx_k_norm_g[l], ret_norm_g[l], w_att_branch[l], w_ret_branch[l], w_out[l],
                   norm2_g[l], w_group_router[l], b_group_router[l], w_expert_router[l],
                   b_expert_router[l], w_exp_gate[l], w_exp_up[l], w_exp_down[l])
    return h
```

```python
import functools
import math

import jax
import jax.numpy as jnp
from jax import lax
from jax.experimental import pallas as pl
from jax.experimental.pallas import tpu as pltpu

D_MODEL = 1024
N_ATT_HEADS = 8
ATT_HEAD_DIM = 128
KV_LORA = 256
IDX_HEADS = 8
IDX_DIM = 64
TOPK_MAX = 256
Q_BLOCK = 128
ATTN_Q = 256
N_RET_HEADS = 4
RET_QK_DIM = 128
RET_V_DIM = 256
RET_CHUNK = 128
N_GROUPS = 4
EXPERTS_PER_GROUP = 4
N_EXPERTS = N_GROUPS * EXPERTS_PER_GROUP
EXPERT_FF = 256
EPS = 1e-6

LANES = 128
KEY_CHUNK = 256
QK_AUG = 256
RET_BATCHES = 8
N_SWEEP_ACC = 4
VT_ROWS = 144
PROJ_W = 7168
C_QATT, C_QIDX, C_CKV, C_KIDX, C_WIDX = 0, 1024, 1536, 1792, 1856
C_RQ, C_RK, C_RV, C_RGATE, C_GATT, C_GRET = 2048, 2560, 3072, 4096, 5120, 6144
ROUTER_W = 128
NEG_BIG = -1e30
VMEM_LIMIT = 56 * 1024 * 1024

_INT_MIN = -(2 ** 31)


def _cparams(sem):
    return pltpu.CompilerParams(dimension_semantics=sem, vmem_limit_bytes=VMEM_LIMIT)


def _proj_kernel(tn, x_ref, g_ref, w_ref, o_ref):
    x = x_ref[...]
    ms = jnp.mean(x * x, axis=-1, keepdims=True)
    n = (x * lax.rsqrt(ms + EPS) * g_ref[...]).astype(jnp.bfloat16)
    for c in range(PROJ_W // tn):
        o_ref[:, c * tn:(c + 1) * tn] = jnp.dot(
            n, w_ref[:, c * tn:(c + 1) * tn], preferred_element_type=jnp.float32).astype(o_ref.dtype)


def _proj(x2, g1, w_p, tm, tn):
    T = x2.shape[0]
    return pl.pallas_call(
        functools.partial(_proj_kernel, tn),
        out_shape=jax.ShapeDtypeStruct((T, PROJ_W), jnp.bfloat16),
        grid=(T // tm,),
        in_specs=[pl.BlockSpec((tm, D_MODEL), lambda i: (i, 0)),
                  pl.BlockSpec((1, D_MODEL), lambda i: (0, 0)),
                  pl.BlockSpec((D_MODEL, PROJ_W), lambda i: (0, 0))],
        out_specs=pl.BlockSpec((tm, PROJ_W), lambda i: (i, 0)),
        compiler_params=_cparams(("parallel",)),
        name="proj",
    )(x2, g1, w_p)


def _prep_kernel(S, q_ref, c_ref, qg_ref, kvg_ref, wkv_ref, kg_ref, ig_ref,
                 qa_ref, ka_ref, vt_ref, kd_ref, w_ref):
    tm = q_ref.shape[0]
    lane = lax.broadcasted_iota(jnp.int32, (tm, LANES), 1)

    qg = qg_ref[...]
    for h in range(N_ATT_HEADS):
        qh = q_ref[:, h * ATT_HEAD_DIM:(h + 1) * ATT_HEAD_DIM].astype(jnp.float32)
        ms = jnp.mean(qh * qh, axis=-1, keepdims=True)
        qa_ref[:, h * ATT_HEAD_DIM:(h + 1) * ATT_HEAD_DIM] = (
            qh * lax.rsqrt(ms + EPS) * qg * (ATT_HEAD_DIM ** -0.5)).astype(jnp.bfloat16)

    c = c_ref[:, 0:KV_LORA].astype(jnp.float32)
    ms = jnp.mean(c * c, axis=-1, keepdims=True)
    cn = (c * lax.rsqrt(ms + EPS) * kvg_ref[...]).astype(jnp.bfloat16)
    kv = jnp.dot(cn, wkv_ref[...], preferred_element_type=jnp.float32)
    k = kv[:, :ATT_HEAD_DIM]
    ms = jnp.mean(k * k, axis=-1, keepdims=True)
    ka_ref[:, 0:ATT_HEAD_DIM] = (k * lax.rsqrt(ms + EPS) * kg_ref[...]).astype(jnp.bfloat16)
    base = (pl.program_id(0) % (S // tm)) * tm
    pos = base + lax.broadcasted_iota(jnp.int32, (tm, LANES), 0)
    pos_lo = pos & 255
    ka_ref[:, ATT_HEAD_DIM:] = jnp.where(
        lane == 0, pos - pos_lo, jnp.where(lane == 1, pos_lo, 0)).astype(jnp.float32).astype(jnp.bfloat16)
    v = kv[:, ATT_HEAD_DIM:]
    ones_row = jnp.where(lax.broadcasted_iota(jnp.int32, (VT_ROWS - ATT_HEAD_DIM, KEY_CHUNK), 0) == 0,
                         1.0, 0.0).astype(jnp.bfloat16)
    for u in range(tm // KEY_CHUNK):
        vt_ref[u, 0:ATT_HEAD_DIM, :] = v[u * KEY_CHUNK:(u + 1) * KEY_CHUNK, :].T.astype(jnp.bfloat16)
        vt_ref[u, ATT_HEAD_DIM:, :] = ones_row

    blk = c_ref[:, C_KIDX - C_CKV:C_KIDX - C_CKV + LANES].astype(jnp.float32)
    is_k = lane < IDX_DIM
    ms = jnp.sum(jnp.where(is_k, blk * blk, 0.0), axis=-1, keepdims=True) * (1.0 / IDX_DIM)
    kn = jnp.where(is_k, blk * lax.rsqrt(ms + EPS) * ig_ref[...], 0.0)
    kd_ref[...] = (kn + pltpu.roll(kn, IDX_DIM, 1)).astype(jnp.bfloat16)
    w_ref[...] = blk * ((IDX_HEADS ** -0.5) * (IDX_DIM ** -0.5))


def _prep(proj, q_norm_g, kv_norm_g, w_kv_up, k_norm_g, idx_g_pad, tm, S):
    T = proj.shape[0]
    row = lambda i: (i, 0)
    const = lambda i: (0, 0)
    return pl.pallas_call(
        functools.partial(_prep_kernel, S),
        out_shape=(jax.ShapeDtypeStruct((T, D_MODEL), jnp.bfloat16),
                   jax.ShapeDtypeStruct((T, QK_AUG), jnp.bfloat16),
                   jax.ShapeDtypeStruct((T // KEY_CHUNK, VT_ROWS, KEY_CHUNK), jnp.bfloat16),
                   jax.ShapeDtypeStruct((T, LANES), jnp.bfloat16),
                   jax.ShapeDtypeStruct((T, LANES), jnp.float32)),
        grid=(T // tm,),
        in_specs=[pl.BlockSpec((tm, 1024), lambda i: (i, C_QATT // 1024)),
                  pl.BlockSpec((tm, 512), lambda i: (i, C_CKV // 512)),
                  pl.BlockSpec((1, ATT_HEAD_DIM), const),
                  pl.BlockSpec((1, KV_LORA), const),
                  pl.BlockSpec((KV_LORA, 2 * ATT_HEAD_DIM), const),
                  pl.BlockSpec((1, ATT_HEAD_DIM), const),
                  pl.BlockSpec((1, LANES), const)],
        out_specs=(pl.BlockSpec((tm, D_MODEL), row),
                   pl.BlockSpec((tm, QK_AUG), row),
                   pl.BlockSpec((tm // KEY_CHUNK, VT_ROWS, KEY_CHUNK), lambda i: (i, 0, 0)),
                   pl.BlockSpec((tm, LANES), row),
                   pl.BlockSpec((tm, LANES), row)),
        compiler_params=_cparams(("parallel",)),
        name="prep",
    )(proj, proj, q_norm_g, kv_norm_g, w_kv_up, k_norm_g, idx_g_pad)


def _attn_kernel(n_sel, qa_ref, qi_ref, w_ref, kd_ref, ka_ref, vt_ref, o_ref,
                 sc_ref, pre_ref, cnt_ref, qs_ref, m_ref, acc_ref):
    j = pl.program_id(1)
    Q, KC = ATTN_Q, KEY_CHUNK
    nch = (j * Q + Q + KC - 1) // KC
    q_pos = j * Q + lax.broadcasted_iota(jnp.int32, (1, Q), 1)
    row_in_chunk = lax.broadcasted_iota(jnp.int32, (KC, Q), 0)
    nt = (((1,), (1,)), ((), ()))

    lane = lax.broadcasted_iota(jnp.int32, (Q, LANES), 1)
    for p in range(IDX_HEADS // 2):
        qp = qi_ref[:, p * LANES:(p + 1) * LANES].astype(jnp.float32)
        qs_ref[p, 0:Q, :] = jnp.where(lane < IDX_DIM, qp, 0.0).astype(jnp.bfloat16)
        qs_ref[p, Q:2 * Q, :] = jnp.where(lane >= IDX_DIM, qp, 0.0).astype(jnp.bfloat16)
    w_t = w_ref[...].T

    def score_chunk(c, carry):
        r0 = pl.multiple_of(c * KC, KC)
        kd = kd_ref[pl.ds(r0, KC), :]
        s = jnp.zeros((KC, Q), jnp.float32)
        for p in range(IDX_HEADS // 2):
            d = lax.dot_general(kd, qs_ref[p], nt, preferred_element_type=jnp.float32)
            h0 = IDX_DIM + 2 * p
            s = (s + w_t[h0:h0 + 1, :] * jnp.maximum(d[:, :Q], 0.0)
                 + w_t[h0 + 1:h0 + 2, :] * jnp.maximum(d[:, Q:], 0.0))
        sc_ref[pl.ds(r0, KC), :] = jnp.where(r0 + row_in_chunk <= q_pos, s, -jnp.inf)
        return carry

    lax.fori_loop(0, nch, score_chunk, 0)

    k_row = jnp.minimum(q_pos + 1, n_sel).astype(jnp.float32)

    def tree_sum(parts):
        while len(parts) > 1:
            odd = parts[-1:] if len(parts) % 2 else []
            parts = [a + b for a, b in zip(parts[0::2], parts[1::2])] + odd
        return parts[0]

    def count_rows(pred):
        def body(c, acc):
            r0 = pl.multiple_of(c * KC, KC)
            hit = jnp.where(pred(sc_ref[pl.ds(r0, KC), :]), 1.0, 0.0)
            return acc + tree_sum([hit[u * 8:(u + 1) * 8, :] for u in range(KC // 8)])

        acc = lax.fori_loop(0, nch, body, jnp.zeros((8, Q), jnp.float32))
        return jnp.sum(acc, axis=0, keepdims=True)

    def key_to_float(u):
        key = u ^ _INT_MIN
        bits = jnp.where(key >= 0, key, key ^ 0x7FFFFFFF)
        return lax.bitcast_convert_type(bits, jnp.float32)

    def bisect(n):
        def count_ge(cand):
            cand_b = jnp.broadcast_to(cand, (8, Q))
            accs = [jnp.zeros((8, Q), jnp.float32) for _ in range(N_SWEEP_ACC)]
            for u, r in enumerate(range(0, n * KC, 8)):
                accs[u % N_SWEEP_ACC] = accs[u % N_SWEEP_ACC] + jnp.where(sc_ref[r:r + 8, :] >= cand_b, 1.0, 0.0)
            return jnp.sum(tree_sum(accs), axis=0, keepdims=True)

        def bit_body(i, carry):
            prefix, cnt_best = carry
            cand_u = prefix | lax.shift_left(jnp.int32(1), 31 - i)
            cnt = count_ge(key_to_float(cand_u))
            take = cnt >= k_row
            return jnp.where(take, cand_u, prefix), jnp.where(take, cnt, cnt_best)

        prefix, cnt_best = lax.fori_loop(0, 32, bit_body, (jnp.zeros((1, Q), jnp.int32), k_row))
        pre_ref[...] = jnp.broadcast_to(prefix, (8, Q))
        cnt_ref[...] = jnp.broadcast_to(cnt_best, (8, Q))

    for n in range(1, sc_ref.shape[0] // KC + 1):
        pl.when(nch == n)(functools.partial(bisect, n))
    prefix = pre_ref[0:1, :]
    cnt_best = cnt_ref[0:1, :]
    thr = key_to_float(prefix)

    @pl.when(jnp.max(cnt_best - k_row) > 0.0)
    def _():
        need = k_row - count_rows(lambda blk: blk > thr)
        r_i = lax.broadcasted_iota(jnp.int32, (KC, KC), 0)
        c_i = lax.broadcasted_iota(jnp.int32, (KC, KC), 1)
        tri = jnp.where(c_i <= r_i, 1.0, 0.0).astype(jnp.bfloat16)

        def drop_body(c, seen):
            r0 = pl.multiple_of(c * KC, KC)
            blk = sc_ref[pl.ds(r0, KC), :]
            eq = blk == thr
            eq_f = jnp.where(eq, 1.0, 0.0)
            incl = jnp.dot(tri, eq_f.astype(jnp.bfloat16), preferred_element_type=jnp.float32)
            rank = seen + incl - eq_f
            sc_ref[pl.ds(r0, KC), :] = jnp.where(eq & (rank >= need), -jnp.inf, blk)
            return seen + jnp.sum(eq_f, axis=0, keepdims=True)

        lax.fori_loop(0, nch, drop_body, jnp.zeros((1, Q), jnp.float32))

    m_ref[...] = jnp.full(m_ref.shape, NEG_BIG, jnp.float32)
    acc_ref[...] = jnp.zeros(acc_ref.shape, jnp.float32)
    QH = Q_BLOCK
    n_half = Q // QH
    lane_h = lax.broadcasted_iota(jnp.int32, (QH, LANES), 1)
    slope_lanes = [jnp.where(lane_h < 2, 2.0 ** (-8.0 * (h + 1) / N_ATT_HEADS), 0.0).astype(jnp.bfloat16)
                   for h in range(N_ATT_HEADS)]

    def att_chunk(c, carry):
        r0 = pl.multiple_of(c * KC, KC)
        ka = ka_ref[pl.ds(r0, KC), :]
        vt = vt_ref[c]
        mask_bias = jnp.where(sc_ref[pl.ds(r0, KC), :] >= thr, 0.0, NEG_BIG)
        for g in range(n_half):
            rows = slice(g * QH, (g + 1) * QH)
            mb = mask_bias[:, rows]
            for p in range(N_ATT_HEADS // 2):
                q_pair = jnp.concatenate(
                    [jnp.concatenate([qa_ref[rows, h * ATT_HEAD_DIM:(h + 1) * ATT_HEAD_DIM], slope_lanes[h]], axis=1)
                     for h in (2 * p, 2 * p + 1)], axis=0)
                lg = lax.dot_general(ka, q_pair, nt, preferred_element_type=jnp.float32)
                ps, alphas = [], []
                for hh in range(2):
                    h = 2 * p + hh
                    lgh = lg[:, hh * QH:(hh + 1) * QH] + mb
                    m_old = m_ref[g, h][0:1, :]
                    m_new = jnp.maximum(m_old, jnp.max(lgh, axis=0, keepdims=True))
                    alphas.append(jnp.exp(m_old - m_new))
                    ps.append(jnp.exp(lgh - m_new).astype(jnp.bfloat16))
                    m_ref[g, h] = jnp.broadcast_to(m_new, (8, QH))
                pv = jnp.dot(vt, jnp.concatenate(ps, axis=1), preferred_element_type=jnp.float32)
                acc_ref[g, p] = acc_ref[g, p] * jnp.concatenate(alphas, axis=1) + pv
        return carry

    lax.fori_loop(0, nch, att_chunk, 0)

    for g in range(n_half):
        for h in range(N_ATT_HEADS):
            a = acc_ref[g, h // 2][:, (h % 2) * QH:(h % 2 + 1) * QH]
            out_t = a[0:ATT_HEAD_DIM, :] / a[ATT_HEAD_DIM:ATT_HEAD_DIM + 1, :]
            o_ref[g * QH:(g + 1) * QH, h * ATT_HEAD_DIM:(h + 1) * ATT_HEAD_DIM] = out_t.T.astype(o_ref.dtype)


def _attn(qa, proj, widx, kd, ka, vt, B, S, n_sel):
    T = B * S
    nb = S // ATTN_Q
    n_kc = S // KEY_CHUNK
    qrow = lambda b, j: (b * nb + j, 0)
    per_b = lambda b, j: (b, 0)
    return pl.pallas_call(
        functools.partial(_attn_kernel, n_sel),
        out_shape=jax.ShapeDtypeStruct((T, D_MODEL), jnp.bfloat16),
        grid=(B, nb),
        in_specs=[pl.BlockSpec((ATTN_Q, D_MODEL), qrow),
                  pl.BlockSpec((ATTN_Q, 512), lambda b, j: (b * nb + j, C_QIDX // 512)),
                  pl.BlockSpec((ATTN_Q, LANES), qrow),
                  pl.BlockSpec((S, LANES), per_b),
                  pl.BlockSpec((S, QK_AUG), per_b),
                  pl.BlockSpec((n_kc, VT_ROWS, KEY_CHUNK), lambda b, j: (b, 0, 0))],
        out_specs=pl.BlockSpec((ATTN_Q, D_MODEL), qrow),
        scratch_shapes=[pltpu.VMEM((S, ATTN_Q), jnp.float32),
                        pltpu.VMEM((8, ATTN_Q), jnp.int32),
                        pltpu.VMEM((8, ATTN_Q), jnp.float32),
                        pltpu.VMEM((IDX_HEADS // 2, 2 * ATTN_Q, LANES), jnp.bfloat16),
                        pltpu.VMEM((ATTN_Q // Q_BLOCK, N_ATT_HEADS, 8, Q_BLOCK), jnp.float32),
                        pltpu.VMEM((ATTN_Q // Q_BLOCK, N_ATT_HEADS // 2, VT_ROWS, 2 * Q_BLOCK), jnp.float32)],
        compiler_params=_cparams(("parallel", "arbitrary")),
        name="attn",
    )(qa, proj, widx, kd, ka, vt)


def _ret_kernel(rq_ref, rk_ref, rv_ref, gate_ref, g_ref, o_ref, state_ref):
    C = RET_CHUNK

    @pl.when(pl.program_id(1) == 0)
    def _():
        state_ref[...] = jnp.zeros(state_ref.shape, jnp.float32)

    ii = lax.broadcasted_iota(jnp.int32, (C, C), 0)
    jj = lax.broadcasted_iota(jnp.int32, (C, C), 1)
    rel = (ii - jj).astype(jnp.float32)
    pos = lax.broadcasted_iota(jnp.int32, (C, 1), 0).astype(jnp.float32)
    for h in range(N_RET_HEADS):
        log_g = math.log(1.0 - 2.0 ** (-5.0 - h))
        decay_intra = jnp.where(rel >= 0, jnp.exp(rel * log_g), 0.0)
        q_decay = jnp.exp((pos + 1.0) * log_g)
        k_decay = jnp.exp((C - 1.0 - pos) * log_g)
        chunk_decay = math.exp(C * log_g)
        sl = slice(h * RET_V_DIM, (h + 1) * RET_V_DIM)
        for b in range(rq_ref.shape[0]):
            q = rq_ref[b, :, h * RET_QK_DIM:(h + 1) * RET_QK_DIM].astype(jnp.float32)
            k = rk_ref[b, :, h * RET_QK_DIM:(h + 1) * RET_QK_DIM].astype(jnp.float32) * (RET_QK_DIM ** -0.5)
            v = rv_ref[b, :, sl]
            state = state_ref[b, h]

            qk = lax.dot_general(q.astype(jnp.bfloat16), k.astype(jnp.bfloat16),
                                 (((1,), (1,)), ((), ())), preferred_element_type=jnp.float32)
            intra = (qk * decay_intra).astype(jnp.bfloat16)
            o = (jnp.dot(intra, v, preferred_element_type=jnp.float32)
                 + jnp.dot((q * q_decay).astype(jnp.bfloat16), state.astype(jnp.bfloat16),
                           preferred_element_type=jnp.float32))
            kd_t = (k * k_decay).T.astype(jnp.bfloat16)
            state_ref[b, h] = chunk_decay * state + jnp.dot(kd_t, v, preferred_element_type=jnp.float32)

            mu = jnp.mean(o, axis=-1, keepdims=True)
            var = jnp.mean(jnp.square(o - mu), axis=-1, keepdims=True)
            y = (o - mu) * lax.rsqrt(var + EPS) * g_ref[:, sl]
            gate = gate_ref[b, :, sl].astype(jnp.float32)
            o_ref[b, :, sl] = (gate * jax.nn.sigmoid(gate) * y).astype(o_ref.dtype)


def _ret(proj, ret_g, B, S):
    n = S // RET_CHUNK
    C = RET_CHUNK
    nbt = RET_BATCHES if B % RET_BATCHES == 0 else 1
    proj3 = proj.reshape(B, S, PROJ_W)
    out = pl.pallas_call(
        _ret_kernel,
        out_shape=jax.ShapeDtypeStruct((B, S, N_RET_HEADS * RET_V_DIM), jnp.bfloat16),
        grid=(B // nbt, n),
        in_specs=[pl.BlockSpec((nbt, C, 512), lambda b, i: (b, i, C_RQ // 512)),
                  pl.BlockSpec((nbt, C, 512), lambda b, i: (b, i, C_RK // 512)),
                  pl.BlockSpec((nbt, C, 1024), lambda b, i: (b, i, C_RV // 1024)),
                  pl.BlockSpec((nbt, C, 1024), lambda b, i: (b, i, C_RGATE // 1024)),
                  pl.BlockSpec((1, 1024), lambda b, i: (0, 0))],
        out_specs=pl.BlockSpec((nbt, C, 1024), lambda b, i: (b, i, 0)),
        scratch_shapes=[pltpu.VMEM((nbt, N_RET_HEADS, RET_QK_DIM, RET_V_DIM), jnp.float32)],
        compiler_params=_cparams(("parallel", "arbitrary")),
        name="ret",
    )(proj3, proj3, proj3, proj3, ret_g)
    return out.reshape(B * S, N_RET_HEADS * RET_V_DIM)


def _merge_kernel(att_ref, ret_ref, ga_ref, gr_ref, x_ref, wa_ref, wr_ref, wo_ref,
                  g2_ref, wrt_ref, brt_ref, h_ref, n2_ref, comb_ref):
    y_att = jnp.dot(att_ref[...], wa_ref[...], preferred_element_type=jnp.float32)
    y_ret = jnp.dot(ret_ref[...], wr_ref[...], preferred_element_type=jnp.float32)
    mixed = (jax.nn.sigmoid(ga_ref[...].astype(jnp.float32)) * y_att
             + jax.nn.sigmoid(gr_ref[...].astype(jnp.float32)) * y_ret)
    h = x_ref[...] + jnp.dot(mixed.astype(jnp.bfloat16), wo_ref[...],
                             preferred_element_type=jnp.float32)
    h_ref[...] = h
    ms = jnp.mean(h * h, axis=-1, keepdims=True)
    n2 = (h * lax.rsqrt(ms + EPS) * g2_ref[...]).astype(jnp.bfloat16)
    n2_ref[...] = n2

    logits = jnp.dot(n2, wrt_ref[...], preferred_element_type=jnp.float32) + brt_ref[...]
    lane = lax.broadcasted_iota(jnp.int32, logits.shape, 1)
    big = jnp.int32(ROUTER_W)
    is_g = lane < N_GROUPS
    g_max = jnp.max(jnp.where(is_g, logits, -jnp.inf), axis=-1, keepdims=True)
    g_sel = jnp.min(jnp.where(is_g & (logits == g_max), lane, big), axis=-1, keepdims=True)
    g_w = 1.0 / jnp.sum(jnp.where(is_g, jnp.exp(logits - g_max), 0.0), axis=-1, keepdims=True)
    lo = N_GROUPS + g_sel * EXPERTS_PER_GROUP
    in_grp = (lane >= lo) & (lane < lo + EXPERTS_PER_GROUP)
    v1 = jnp.max(jnp.where(in_grp, logits, -jnp.inf), axis=-1, keepdims=True)
    i1 = jnp.min(jnp.where(in_grp & (logits == v1), lane, big), axis=-1, keepdims=True)
    rest = in_grp & (lane != i1)
    v2 = jnp.max(jnp.where(rest, logits, -jnp.inf), axis=-1, keepdims=True)
    i2 = jnp.min(jnp.where(rest & (logits == v2), lane, big), axis=-1, keepdims=True)
    e2 = jnp.exp(v2 - v1)
    p1 = g_w / (1.0 + e2)
    p2 = p1 * e2
    comb_ref[...] = jnp.where(lane == i1, p1, 0.0) + jnp.where(lane == i2, p2, 0.0)


def _merge(att, gret, proj, x2, wa, wr, wo, g2, w_router, b_router, tm):
    T = x2.shape[0]
    row = lambda i: (i, 0)
    const = lambda i: (0, 0)
    return pl.pallas_call(
        _merge_kernel,
        out_shape=(jax.ShapeDtypeStruct((T, D_MODEL), jnp.float32),
                   jax.ShapeDtypeStruct((T, D_MODEL), jnp.bfloat16),
                   jax.ShapeDtypeStruct((T, ROUTER_W), jnp.float32)),
        grid=(T // tm,),
        in_specs=[pl.BlockSpec((tm, 1024), row),
                  pl.BlockSpec((tm, 1024), row),
                  pl.BlockSpec((tm, 1024), lambda i: (i, C_GATT // 1024)),
                  pl.BlockSpec((tm, 1024), lambda i: (i, C_GRET // 1024)),
                  pl.BlockSpec((tm, 1024), row),
                  pl.BlockSpec((1024, 1024), const),
                  pl.BlockSpec((1024, 1024), const),
                  pl.BlockSpec((1024, 1024), const),
                  pl.BlockSpec((1, 1024), const),
                  pl.BlockSpec((1024, ROUTER_W), const),
                  pl.BlockSpec((1, ROUTER_W), const)],
        out_specs=(pl.BlockSpec((tm, 1024), row),
                   pl.BlockSpec((tm, 1024), row),
                   pl.BlockSpec((tm, ROUTER_W), row)),
        compiler_params=_cparams(("parallel",)),
        name="merge",
    )(att, gret, proj, proj, x2, wa, wr, wo, g2, w_router, b_router)


def _moe_kernel(n2_ref, comb_ref, h_ref, wg_ref, wu_ref, wd_ref, o_ref):
    n2 = n2_ref[...]
    comb = comb_ref[...]
    lane = lax.broadcasted_iota(jnp.int32, comb.shape, 1)
    y = h_ref[...]
    for g in range(N_GROUPS):
        acts = []
        for e in range(EXPERTS_PER_GROUP):
            ex = g * EXPERTS_PER_GROUP + e
            c = jnp.sum(jnp.where(lane == N_GROUPS + ex, comb, 0.0), axis=-1, keepdims=True)
            hg = jnp.dot(n2, wg_ref[ex], preferred_element_type=jnp.float32)
            hu = jnp.dot(n2, wu_ref[ex], preferred_element_type=jnp.float32)
            acts.append((hg * jax.nn.sigmoid(hg) * hu * c).astype(jnp.bfloat16))
        y = y + jnp.dot(jnp.concatenate(acts, axis=1), wd_ref[g], preferred_element_type=jnp.float32)
    o_ref[...] = y


def _moe(n2, comb, h1, w_g, w_u, w_d, tm):
    T = n2.shape[0]
    E, F = EXPERTS_PER_GROUP, EXPERT_FF
    row = lambda i: (i, 0)
    whole = lambda i: (0, 0, 0)
    resident = dict(pipeline_mode=pl.Buffered(1))
    return pl.pallas_call(
        _moe_kernel,
        out_shape=jax.ShapeDtypeStruct((T, D_MODEL), jnp.float32),
        grid=(T // tm,),
        in_specs=[pl.BlockSpec((tm, D_MODEL), row),
                  pl.BlockSpec((tm, ROUTER_W), row),
                  pl.BlockSpec((tm, D_MODEL), row),
                  pl.BlockSpec((N_EXPERTS, D_MODEL, F), whole, **resident),
                  pl.BlockSpec((N_EXPERTS, D_MODEL, F), whole, **resident),
                  pl.BlockSpec((N_GROUPS, E * F, D_MODEL), whole, **resident)],
        out_specs=pl.BlockSpec((tm, D_MODEL), row),
        compiler_params=_cparams(("parallel",)),
        name="moe",
    )(n2, comb, h1, w_g, w_u, w_d)


def _relayout_w_in(w):
    pad = jnp.zeros((w.shape[0], C_RQ - (C_WIDX + IDX_HEADS)), w.dtype)
    return jnp.concatenate(
        [w[:, 0:1024],
         w[:, 1280:1792],
         w[:, 1024:1280],
         w[:, 1792:1864],
         pad,
         w[:, 1864:]], axis=1)


def _layer(h, norm1_g, w_in, kv_norm_g, w_kv_up, q_norm_g, k_norm_g, idx_k_norm_g,
           ret_norm_g, w_att_branch, w_ret_branch, w_out, norm2_g, w_group_router,
           b_group_router, w_expert_router, b_expert_router, w_exp_gate, w_exp_up, w_exp_down):
    B, S, D = h.shape
    assert D == D_MODEL and S % KEY_CHUNK == 0 and S % Q_BLOCK == 0 and S <= 65536
    T = B * S
    n_sel = min(TOPK_MAX, S // 4)
    tm = 512 if S % 512 == 0 else 256
    bf = jnp.bfloat16
    f32 = jnp.float32

    x2 = h.reshape(T, D)
    w_p = _relayout_w_in(w_in.astype(bf))
    proj = _proj(x2, norm1_g.reshape(1, D).astype(f32), w_p, tm, 1024)

    idx_g_pad = jnp.concatenate([idx_k_norm_g.astype(f32), jnp.zeros((LANES - IDX_DIM,), f32)])
    qa, ka, vt, kd, widx = _prep(
        proj, q_norm_g.reshape(1, -1).astype(f32), kv_norm_g.reshape(1, -1).astype(f32),
        w_kv_up.astype(bf), k_norm_g.reshape(1, -1).astype(f32), idx_g_pad.reshape(1, LANES), tm, S)

    att = _attn(qa, proj, widx, kd, ka, vt, B, S, n_sel)
    gret = _ret(proj, ret_norm_g.reshape(1, -1).astype(f32), B, S)

    w_router = jnp.concatenate(
        [w_group_router, w_expert_router,
         jnp.zeros((D, ROUTER_W - N_GROUPS - N_EXPERTS), w_group_router.dtype)], axis=1).astype(bf)
    b_router = jnp.concatenate(
        [b_group_router, b_expert_router,
         jnp.zeros((ROUTER_W - N_GROUPS - N_EXPERTS,), b_group_router.dtype)]).reshape(1, ROUTER_W)
    h1, n2, comb = _merge(att, gret, proj, x2, w_att_branch.astype(bf), w_ret_branch.astype(bf),
                          w_out.astype(bf), norm2_g.reshape(1, D).astype(f32),
                          w_router, b_router.astype(f32), tm)

    w_d = w_exp_down.astype(bf).reshape(N_GROUPS, EXPERTS_PER_GROUP * EXPERT_FF, D)
    out = _moe(n2, comb, h1, w_exp_gate.astype(bf), w_exp_up.astype(bf), w_d, tm)
    return out.reshape(B, S, D)


def kernel(x, norm1_g, w_in, kv_norm_g, w_kv_up, q_norm_g, k_norm_g, idx_k_norm_g, ret_norm_g,
           w_att_branch, w_ret_branch, w_out, norm2_g, w_group_router, b_group_router,
           w_expert_router, b_expert_router, w_exp_gate, w_exp_up, w_exp_down):
    h = x
    for l in range(norm1_g.shape[0]):
        h = _layer(h, norm1_g[l], w_in[l], kv_norm_g[l], w_kv_up[l], q_norm_g[l], k_norm_g[l],
                   idx_k_norm_g[l], ret_norm_g[l], w_att_branch[l], w_ret_branch[l], w_out[l],
                   norm2_g[l], w_group_router[l], b_group_router[l], w_expert_router[l],
                   b_expert_router[l], w_exp_gate[l], w_exp_up[l], w_exp_down[l])
    return h
```

```python
import functools
import math

import jax
import jax.numpy as jnp
from jax import lax
from jax.experimental import pallas as pl
from jax.experimental.pallas import tpu as pltpu

D_MODEL = 1024
N_ATT_HEADS = 8
ATT_HEAD_DIM = 128
KV_LORA = 256
IDX_HEADS = 8
IDX_DIM = 64
TOPK_MAX = 256
Q_BLOCK = 128
ATTN_Q = 256
N_RET_HEADS = 4
RET_QK_DIM = 128
RET_V_DIM = 256
RET_CHUNK = 128
N_GROUPS = 4
EXPERTS_PER_GROUP = 4
N_EXPERTS = N_GROUPS * EXPERTS_PER_GROUP
EXPERT_FF = 256
EPS = 1e-6

LANES = 128
KEY_CHUNK = 256
QK_AUG = 256
RET_BATCHES = 8
N_SWEEP_ACC = 4
VT_ROWS = 144
PROJ_W = 7168
C_QATT, C_QIDX, C_CKV, C_KIDX, C_WIDX = 0, 1024, 1536, 1792, 1856
C_RQ, C_RK, C_RV, C_RGATE, C_GATT, C_GRET = 2048, 2560, 3072, 4096, 5120, 6144
ROUTER_W = 128
NEG_BIG = -1e30
VMEM_LIMIT = 56 * 1024 * 1024

_INT_MIN = -(2 ** 31)


def _cparams(sem):
    return pltpu.CompilerParams(dimension_semantics=sem, vmem_limit_bytes=VMEM_LIMIT)


def _proj_kernel(tn, x_ref, g_ref, wa_ref, wb_ref, o_ref):
    x = x_ref[...]
    ms = jnp.mean(x * x, axis=-1, keepdims=True)
    n = (x * lax.rsqrt(ms + EPS) * g_ref[...]).astype(jnp.bfloat16)
    n_head = C_RQ // tn
    for c in range(PROJ_W // tn):
        w = (wa_ref[:, c * tn:(c + 1) * tn] if c < n_head
             else wb_ref[:, (c - n_head) * tn:(c - n_head + 1) * tn])
        o_ref[:, c * tn:(c + 1) * tn] = jnp.dot(
            n, w, preferred_element_type=jnp.float32).astype(o_ref.dtype)


def _proj(x2, g1, w_head, w_tail, tm, tn):
    T = x2.shape[0]
    return pl.pallas_call(
        functools.partial(_proj_kernel, tn),
        out_shape=jax.ShapeDtypeStruct((T, PROJ_W), jnp.bfloat16),
        grid=(T // tm,),
        in_specs=[pl.BlockSpec((tm, D_MODEL), lambda i: (i, 0)),
                  pl.BlockSpec((1, D_MODEL), lambda i: (0, 0)),
                  pl.BlockSpec((D_MODEL, C_RQ), lambda i: (0, 0)),
                  pl.BlockSpec((D_MODEL, PROJ_W - C_RQ), lambda i: (0, 0))],
        out_specs=pl.BlockSpec((tm, PROJ_W), lambda i: (i, 0)),
        compiler_params=_cparams(("parallel",)),
        name="proj",
    )(x2, g1, w_head, w_tail)


def _prep_kernel(S, q_ref, c_ref, qg_ref, kvg_ref, wkv_ref, kg_ref, ig_ref,
                 qa_ref, ka_ref, vt_ref, kd_ref, w_ref):
    tm = q_ref.shape[0]
    lane = lax.broadcasted_iota(jnp.int32, (tm, LANES), 1)

    qg = qg_ref[...]
    for h in range(N_ATT_HEADS):
        qh = q_ref[:, h * ATT_HEAD_DIM:(h + 1) * ATT_HEAD_DIM].astype(jnp.float32)
        ms = jnp.mean(qh * qh, axis=-1, keepdims=True)
        qa_ref[:, h * ATT_HEAD_DIM:(h + 1) * ATT_HEAD_DIM] = (
            qh * lax.rsqrt(ms + EPS) * qg * (ATT_HEAD_DIM ** -0.5)).astype(jnp.bfloat16)

    c = c_ref[:, 0:KV_LORA].astype(jnp.float32)
    ms = jnp.mean(c * c, axis=-1, keepdims=True)
    cn = (c * lax.rsqrt(ms + EPS) * kvg_ref[...]).astype(jnp.bfloat16)
    kv = jnp.dot(cn, wkv_ref[...], preferred_element_type=jnp.float32)
    k = kv[:, :ATT_HEAD_DIM]
    ms = jnp.mean(k * k, axis=-1, keepdims=True)
    ka_ref[:, 0:ATT_HEAD_DIM] = (k * lax.rsqrt(ms + EPS) * kg_ref[...]).astype(jnp.bfloat16)
    base = (pl.program_id(0) % (S // tm)) * tm
    pos = base + lax.broadcasted_iota(jnp.int32, (tm, LANES), 0)
    pos_lo = pos & 255
    ka_ref[:, ATT_HEAD_DIM:] = jnp.where(
        lane == 0, pos - pos_lo, jnp.where(lane == 1, pos_lo, 0)).astype(jnp.float32).astype(jnp.bfloat16)
    v = kv[:, ATT_HEAD_DIM:]
    ones_row = jnp.where(lax.broadcasted_iota(jnp.int32, (VT_ROWS - ATT_HEAD_DIM, KEY_CHUNK), 0) == 0,
                         1.0, 0.0).astype(jnp.bfloat16)
    for u in range(tm // KEY_CHUNK):
        vt_ref[u, 0:ATT_HEAD_DIM, :] = v[u * KEY_CHUNK:(u + 1) * KEY_CHUNK, :].T.astype(jnp.bfloat16)
        vt_ref[u, ATT_HEAD_DIM:, :] = ones_row

    blk = c_ref[:, C_KIDX - C_CKV:C_KIDX - C_CKV + LANES].astype(jnp.float32)
    is_k = lane < IDX_DIM
    ms = jnp.sum(jnp.where(is_k, blk * blk, 0.0), axis=-1, keepdims=True) * (1.0 / IDX_DIM)
    kn = jnp.where(is_k, blk * lax.rsqrt(ms + EPS) * ig_ref[...], 0.0)
    kd_ref[...] = (kn + pltpu.roll(kn, IDX_DIM, 1)).astype(jnp.bfloat16)
    w_ref[...] = blk * ((IDX_HEADS ** -0.5) * (IDX_DIM ** -0.5))


def _prep(proj, q_norm_g, kv_norm_g, w_kv_up, k_norm_g, idx_g_pad, tm, S):
    T = proj.shape[0]
    row = lambda i: (i, 0)
    const = lambda i: (0, 0)
    return pl.pallas_call(
        functools.partial(_prep_kernel, S),
        out_shape=(jax.ShapeDtypeStruct((T, D_MODEL), jnp.bfloat16),
                   jax.ShapeDtypeStruct((T, QK_AUG), jnp.bfloat16),
                   jax.ShapeDtypeStruct((T // KEY_CHUNK, VT_ROWS, KEY_CHUNK), jnp.bfloat16),
                   jax.ShapeDtypeStruct((T, LANES), jnp.bfloat16),
                   jax.ShapeDtypeStruct((T, LANES), jnp.float32)),
        grid=(T // tm,),
        in_specs=[pl.BlockSpec((tm, 1024), lambda i: (i, C_QATT // 1024)),
                  pl.BlockSpec((tm, 512), lambda i: (i, C_CKV // 512)),
                  pl.BlockSpec((1, ATT_HEAD_DIM), const),
                  pl.BlockSpec((1, KV_LORA), const),
                  pl.BlockSpec((KV_LORA, 2 * ATT_HEAD_DIM), const),
                  pl.BlockSpec((1, ATT_HEAD_DIM), const),
                  pl.BlockSpec((1, LANES), const)],
        out_specs=(pl.BlockSpec((tm, D_MODEL), row),
                   pl.BlockSpec((tm, QK_AUG), row),
                   pl.BlockSpec((tm // KEY_CHUNK, VT_ROWS, KEY_CHUNK), lambda i: (i, 0, 0)),
                   pl.BlockSpec((tm, LANES), row),
                   pl.BlockSpec((tm, LANES), row)),
        compiler_params=_cparams(("parallel",)),
        name="prep",
    )(proj, proj, q_norm_g, kv_norm_g, w_kv_up, k_norm_g, idx_g_pad)


def _attn_kernel(n_sel, qa_ref, qi_ref, w_ref, kd_ref, ka_ref, vt_ref, o_ref,
                 sc_ref, pre_ref, cnt_ref, qs_ref, m_ref, acc_ref):
    j = pl.program_id(1)
    Q, KC = ATTN_Q, KEY_CHUNK
    nch = (j * Q + Q + KC - 1) // KC
    q_pos = j * Q + lax.broadcasted_iota(jnp.int32, (1, Q), 1)
    row_in_chunk = lax.broadcasted_iota(jnp.int32, (KC, Q), 0)
    nt = (((1,), (1,)), ((), ()))

    lane = lax.broadcasted_iota(jnp.int32, (Q, LANES), 1)
    for p in range(IDX_HEADS // 2):
        qp = qi_ref[:, p * LANES:(p + 1) * LANES].astype(jnp.float32)
        qs_ref[p, 0:Q, :] = jnp.where(lane < IDX_DIM, qp, 0.0).astype(jnp.bfloat16)
        qs_ref[p, Q:2 * Q, :] = jnp.where(lane >= IDX_DIM, qp, 0.0).astype(jnp.bfloat16)
    w_t = w_ref[...].T

    def score_chunk(c, carry):
        r0 = pl.multiple_of(c * KC, KC)
        kd = kd_ref[pl.ds(r0, KC), :]
        s = jnp.zeros((KC, Q), jnp.float32)
        for p in range(IDX_HEADS // 2):
            d = lax.dot_general(kd, qs_ref[p], nt, preferred_element_type=jnp.float32)
            h0 = IDX_DIM + 2 * p
            s = (s + w_t[h0:h0 + 1, :] * jnp.maximum(d[:, :Q], 0.0)
                 + w_t[h0 + 1:h0 + 2, :] * jnp.maximum(d[:, Q:], 0.0))
        sc_ref[pl.ds(r0, KC), :] = jnp.where(r0 + row_in_chunk <= q_pos, s, -jnp.inf)
        return carry

    lax.fori_loop(0, nch, score_chunk, 0)

    k_row = jnp.minimum(q_pos + 1, n_sel).astype(jnp.float32)

    def tree_sum(parts):
        while len(parts) > 1:
            odd = parts[-1:] if len(parts) % 2 else []
            parts = [a + b for a, b in zip(parts[0::2], parts[1::2])] + odd
        return parts[0]

    def count_rows(pred):
        def body(c, acc):
            r0 = pl.multiple_of(c * KC, KC)
            hit = jnp.where(pred(sc_ref[pl.ds(r0, KC), :]), 1.0, 0.0)
            return acc + tree_sum([hit[u * 8:(u + 1) * 8, :] for u in range(KC // 8)])

        acc = lax.fori_loop(0, nch, body, jnp.zeros((8, Q), jnp.float32))
        return jnp.sum(acc, axis=0, keepdims=True)

    def key_to_float(u):
        key = u ^ _INT_MIN
        bits = jnp.where(key >= 0, key, key ^ 0x7FFFFFFF)
        return lax.bitcast_convert_type(bits, jnp.float32)

    def bisect(n):
        def count_ge(cand):
            cand_b = jnp.broadcast_to(cand, (8, Q))
            accs = [jnp.zeros((8, Q), jnp.float32) for _ in range(N_SWEEP_ACC)]
            for u, r in enumerate(range(0, n * KC, 8)):
                accs[u % N_SWEEP_ACC] = accs[u % N_SWEEP_ACC] + jnp.where(sc_ref[r:r + 8, :] >= cand_b, 1.0, 0.0)
            return jnp.sum(tree_sum(accs), axis=0, keepdims=True)

        def bit_body(i, carry):
            prefix, cnt_best = carry
            cand_u = prefix | lax.shift_left(jnp.int32(1), 31 - i)
            cnt = count_ge(key_to_float(cand_u))
            take = cnt >= k_row
            return jnp.where(take, cand_u, prefix), jnp.where(take, cnt, cnt_best)

        prefix, cnt_best = lax.fori_loop(0, 32, bit_body, (jnp.zeros((1, Q), jnp.int32), k_row))
        pre_ref[...] = jnp.broadcast_to(prefix, (8, Q))
        cnt_ref[...] = jnp.broadcast_to(cnt_best, (8, Q))

    for n in range(1, sc_ref.shape[0] // KC + 1):
        pl.when(nch == n)(functools.partial(bisect, n))
    prefix = pre_ref[0:1, :]
    cnt_best = cnt_ref[0:1, :]
    thr = key_to_float(prefix)

    @pl.when(jnp.max(cnt_best - k_row) > 0.0)
    def _():
        need = k_row - count_rows(lambda blk: blk > thr)
        r_i = lax.broadcasted_iota(jnp.int32, (KC, KC), 0)
        c_i = lax.broadcasted_iota(jnp.int32, (KC, KC), 1)
        tri = jnp.where(c_i <= r_i, 1.0, 0.0).astype(jnp.bfloat16)

        def drop_body(c, seen):
            r0 = pl.multiple_of(c * KC, KC)
            blk = sc_ref[pl.ds(r0, KC), :]
            eq = blk == thr
            eq_f = jnp.where(eq, 1.0, 0.0)
            incl = jnp.dot(tri, eq_f.astype(jnp.bfloat16), preferred_element_type=jnp.float32)
            rank = seen + incl - eq_f
            sc_ref[pl.ds(r0, KC), :] = jnp.where(eq & (rank >= need), -jnp.inf, blk)
            return seen + jnp.sum(eq_f, axis=0, keepdims=True)

        lax.fori_loop(0, nch, drop_body, jnp.zeros((1, Q), jnp.float32))

    m_ref[...] = jnp.full(m_ref.shape, NEG_BIG, jnp.float32)
    acc_ref[...] = jnp.zeros(acc_ref.shape, jnp.float32)
    QH = Q_BLOCK
    n_half = Q // QH
    lane_h = lax.broadcasted_iota(jnp.int32, (QH, LANES), 1)
    slope_lanes = [jnp.where(lane_h < 2, 2.0 ** (-8.0 * (h + 1) / N_ATT_HEADS), 0.0).astype(jnp.bfloat16)
                   for h in range(N_ATT_HEADS)]

    def att_chunk(c, carry):
        r0 = pl.multiple_of(c * KC, KC)
        ka = ka_ref[pl.ds(r0, KC), :]
        vt = vt_ref[c]
        mask_bias = jnp.where(sc_ref[pl.ds(r0, KC), :] >= thr, 0.0, NEG_BIG)
        for g in range(n_half):
            rows = slice(g * QH, (g + 1) * QH)
            mb = mask_bias[:, rows]
            for p in range(N_ATT_HEADS // 2):
                q_pair = jnp.concatenate(
                    [jnp.concatenate([qa_ref[rows, h * ATT_HEAD_DIM:(h + 1) * ATT_HEAD_DIM], slope_lanes[h]], axis=1)
                     for h in (2 * p, 2 * p + 1)], axis=0)
                lg = lax.dot_general(ka, q_pair, nt, preferred_element_type=jnp.float32)
                ps, alphas = [], []
                for hh in range(2):
                    h = 2 * p + hh
                    lgh = lg[:, hh * QH:(hh + 1) * QH] + mb
                    m_old = m_ref[g, h][0:1, :]
                    m_new = jnp.maximum(m_old, jnp.max(lgh, axis=0, keepdims=True))
                    alphas.append(jnp.exp(m_old - m_new))
                    ps.append(jnp.exp(lgh - m_new).astype(jnp.bfloat16))
                    m_ref[g, h] = jnp.broadcast_to(m_new, (8, QH))
                pv = jnp.dot(vt, jnp.concatenate(ps, axis=1), preferred_element_type=jnp.float32)
                acc_ref[g, p] = acc_ref[g, p] * jnp.concatenate(alphas, axis=1) + pv
        return carry

    lax.fori_loop(0, nch, att_chunk, 0)

    for g in range(n_half):
        for h in range(N_ATT_HEADS):
            a = acc_ref[g, h // 2][:, (h % 2) * QH:(h % 2 + 1) * QH]
            out_t = a[0:ATT_HEAD_DIM, :] / a[ATT_HEAD_DIM:ATT_HEAD_DIM + 1, :]
            o_ref[g * QH:(g + 1) * QH, h * ATT_HEAD_DIM:(h + 1) * ATT_HEAD_DIM] = out_t.T.astype(o_ref.dtype)


def _attn(qa, proj, widx, kd, ka, vt, B, S, n_sel):
    T = B * S
    nb = S // ATTN_Q
    n_kc = S // KEY_CHUNK
    qrow = lambda b, j: (b * nb + j, 0)
    per_b = lambda b, j: (b, 0)
    return pl.pallas_call(
        functools.partial(_attn_kernel, n_sel),
        out_shape=jax.ShapeDtypeStruct((T, D_MODEL), jnp.bfloat16),
        grid=(B, nb),
        in_specs=[pl.BlockSpec((ATTN_Q, D_MODEL), qrow),
                  pl.BlockSpec((ATTN_Q, 512), lambda b, j: (b * nb + j, C_QIDX // 512)),
                  pl.BlockSpec((ATTN_Q, LANES), qrow),
                  pl.BlockSpec((S, LANES), per_b),
                  pl.BlockSpec((S, QK_AUG), per_b),
                  pl.BlockSpec((n_kc, VT_ROWS, KEY_CHUNK), lambda b, j: (b, 0, 0))],
        out_specs=pl.BlockSpec((ATTN_Q, D_MODEL), qrow),
        scratch_shapes=[pltpu.VMEM((S, ATTN_Q), jnp.float32),
                        pltpu.VMEM((8, ATTN_Q), jnp.int32),
                        pltpu.VMEM((8, ATTN_Q), jnp.float32),
                        pltpu.VMEM((IDX_HEADS // 2, 2 * ATTN_Q, LANES), jnp.bfloat16),
                        pltpu.VMEM((ATTN_Q // Q_BLOCK, N_ATT_HEADS, 8, Q_BLOCK), jnp.float32),
                        pltpu.VMEM((ATTN_Q // Q_BLOCK, N_ATT_HEADS // 2, VT_ROWS, 2 * Q_BLOCK), jnp.float32)],
        compiler_params=_cparams(("parallel", "arbitrary")),
        name="attn",
    )(qa, proj, widx, kd, ka, vt)


def _ret_kernel(rq_ref, rk_ref, rv_ref, gate_ref, g_ref, o_ref, state_ref):
    C = RET_CHUNK

    @pl.when(pl.program_id(1) == 0)
    def _():
        state_ref[...] = jnp.zeros(state_ref.shape, jnp.float32)

    ii = lax.broadcasted_iota(jnp.int32, (C, C), 0)
    jj = lax.broadcasted_iota(jnp.int32, (C, C), 1)
    rel = (ii - jj).astype(jnp.float32)
    pos = lax.broadcasted_iota(jnp.int32, (C, 1), 0).astype(jnp.float32)
    for h in range(N_RET_HEADS):
        log_g = math.log(1.0 - 2.0 ** (-5.0 - h))
        decay_intra = jnp.where(rel >= 0, jnp.exp(rel * log_g), 0.0)
        q_decay = jnp.exp((pos + 1.0) * log_g)
        k_decay = jnp.exp((C - 1.0 - pos) * log_g)
        chunk_decay = math.exp(C * log_g)
        sl = slice(h * RET_V_DIM, (h + 1) * RET_V_DIM)
        for b in range(rq_ref.shape[0]):
            q = rq_ref[b, :, h * RET_QK_DIM:(h + 1) * RET_QK_DIM].astype(jnp.float32)
            k = rk_ref[b, :, h * RET_QK_DIM:(h + 1) * RET_QK_DIM].astype(jnp.float32) * (RET_QK_DIM ** -0.5)
            v = rv_ref[b, :, sl]
            state = state_ref[b, h]

            qk = lax.dot_general(q.astype(jnp.bfloat16), k.astype(jnp.bfloat16),
                                 (((1,), (1,)), ((), ())), preferred_element_type=jnp.float32)
            intra = (qk * decay_intra).astype(jnp.bfloat16)
            o = (jnp.dot(intra, v, preferred_element_type=jnp.float32)
                 + jnp.dot((q * q_decay).astype(jnp.bfloat16), state.astype(jnp.bfloat16),
                           preferred_element_type=jnp.float32))
            kd_t = (k * k_decay).T.astype(jnp.bfloat16)
            state_ref[b, h] = chunk_decay * state + jnp.dot(kd_t, v, preferred_element_type=jnp.float32)

            mu = jnp.mean(o, axis=-1, keepdims=True)
            var = jnp.mean(jnp.square(o - mu), axis=-1, keepdims=True)
            y = (o - mu) * lax.rsqrt(var + EPS) * g_ref[:, sl]
            gate = gate_ref[b, :, sl].astype(jnp.float32)
            o_ref[b, :, sl] = (gate * jax.nn.sigmoid(gate) * y).astype(o_ref.dtype)


def _ret(proj, ret_g, B, S):
    n = S // RET_CHUNK
    C = RET_CHUNK
    nbt = RET_BATCHES if B % RET_BATCHES == 0 else 1
    proj3 = proj.reshape(B, S, PROJ_W)
    out = pl.pallas_call(
        _ret_kernel,
        out_shape=jax.ShapeDtypeStruct((B, S, N_RET_HEADS * RET_V_DIM), jnp.bfloat16),
        grid=(B // nbt, n),
        in_specs=[pl.BlockSpec((nbt, C, 512), lambda b, i: (b, i, C_RQ // 512)),
                  pl.BlockSpec((nbt, C, 512), lambda b, i: (b, i, C_RK // 512)),
                  pl.BlockSpec((nbt, C, 1024), lambda b, i: (b, i, C_RV // 1024)),
                  pl.BlockSpec((nbt, C, 1024), lambda b, i: (b, i, C_RGATE // 1024)),
                  pl.BlockSpec((1, 1024), lambda b, i: (0, 0))],
        out_specs=pl.BlockSpec((nbt, C, 1024), lambda b, i: (b, i, 0)),
        scratch_shapes=[pltpu.VMEM((nbt, N_RET_HEADS, RET_QK_DIM, RET_V_DIM), jnp.float32)],
        compiler_params=_cparams(("parallel", "arbitrary")),
        name="ret",
    )(proj3, proj3, proj3, proj3, ret_g)
    return out.reshape(B * S, N_RET_HEADS * RET_V_DIM)


def _merge_kernel(att_ref, ret_ref, ga_ref, gr_ref, x_ref, wa_ref, wr_ref, wo_ref,
                  g2_ref, wrt_ref, brt_ref, h_ref, n2_ref, comb_ref):
    y_att = jnp.dot(att_ref[...], wa_ref[...], preferred_element_type=jnp.float32)
    y_ret = jnp.dot(ret_ref[...], wr_ref[...], preferred_element_type=jnp.float32)
    mixed = (jax.nn.sigmoid(ga_ref[...].astype(jnp.float32)) * y_att
             + jax.nn.sigmoid(gr_ref[...].astype(jnp.float32)) * y_ret)
    h = x_ref[...] + jnp.dot(mixed.astype(jnp.bfloat16), wo_ref[...],
                             preferred_element_type=jnp.float32)
    h_ref[...] = h
    ms = jnp.mean(h * h, axis=-1, keepdims=True)
    n2 = (h * lax.rsqrt(ms + EPS) * g2_ref[...]).astype(jnp.bfloat16)
    n2_ref[...] = n2

    logits = jnp.dot(n2, wrt_ref[...], preferred_element_type=jnp.float32) + brt_ref[...]
    lane = lax.broadcasted_iota(jnp.int32, logits.shape, 1)
    big = jnp.int32(ROUTER_W)
    is_g = lane < N_GROUPS
    g_max = jnp.max(jnp.where(is_g, logits, -jnp.inf), axis=-1, keepdims=True)
    g_sel = jnp.min(jnp.where(is_g & (logits == g_max), lane, big), axis=-1, keepdims=True)
    g_w = 1.0 / jnp.sum(jnp.where(is_g, jnp.exp(logits - g_max), 0.0), axis=-1, keepdims=True)
    lo = N_GROUPS + g_sel * EXPERTS_PER_GROUP
    in_grp = (lane >= lo) & (lane < lo + EXPERTS_PER_GROUP)
    v1 = jnp.max(jnp.where(in_grp, logits, -jnp.inf), axis=-1, keepdims=True)
    i1 = jnp.min(jnp.where(in_grp & (logits == v1), lane, big), axis=-1, keepdims=True)
    rest = in_grp & (lane != i1)
    v2 = jnp.max(jnp.where(rest, logits, -jnp.inf), axis=-1, keepdims=True)
    i2 = jnp.min(jnp.where(rest & (logits == v2), lane, big), axis=-1, keepdims=True)
    e2 = jnp.exp(v2 - v1)
    p1 = g_w / (1.0 + e2)
    p2 = p1 * e2
    comb_ref[...] = jnp.where(lane == i1, p1, 0.0) + jnp.where(lane == i2, p2, 0.0)


def _merge(att, gret, proj, x2, wa, wr, wo, g2, w_router, b_router, tm):
    T = x2.shape[0]
    row = lambda i: (i, 0)
    const = lambda i: (0, 0)
    return pl.pallas_call(
        _merge_kernel,
        out_shape=(jax.ShapeDtypeStruct((T, D_MODEL), jnp.float32),
                   jax.ShapeDtypeStruct((T, D_MODEL), jnp.bfloat16),
                   jax.ShapeDtypeStruct((T, ROUTER_W), jnp.float32)),
        grid=(T // tm,),
        in_specs=[pl.BlockSpec((tm, 1024), row),
                  pl.BlockSpec((tm, 1024), row),
                  pl.BlockSpec((tm, 1024), lambda i: (i, C_GATT // 1024)),
                  pl.BlockSpec((tm, 1024), lambda i: (i, C_GRET // 1024)),
                  pl.BlockSpec((tm, 1024), row),
                  pl.BlockSpec((1024, 1024), const),
                  pl.BlockSpec((1024, 1024), const),
                  pl.BlockSpec((1024, 1024), const),
                  pl.BlockSpec((1, 1024), const),
                  pl.BlockSpec((1024, ROUTER_W), const),
                  pl.BlockSpec((1, ROUTER_W), const)],
        out_specs=(pl.BlockSpec((tm, 1024), row),
                   pl.BlockSpec((tm, 1024), row),
                   pl.BlockSpec((tm, ROUTER_W), row)),
        compiler_params=_cparams(("parallel",)),
        name="merge",
    )(att, gret, proj, proj, x2, wa, wr, wo, g2, w_router, b_router)


def _moe_kernel(n2_ref, comb_ref, h_ref, wg_ref, wu_ref, wd_ref, o_ref):
    n2 = n2_ref[...]
    comb = comb_ref[...]
    lane = lax.broadcasted_iota(jnp.int32, comb.shape, 1)
    y = h_ref[...]
    for g in range(N_GROUPS):
        acts = []
        for e in range(EXPERTS_PER_GROUP):
            ex = g * EXPERTS_PER_GROUP + e
            c = jnp.sum(jnp.where(lane == N_GROUPS + ex, comb, 0.0), axis=-1, keepdims=True)
            hg = jnp.dot(n2, wg_ref[ex], preferred_element_type=jnp.float32)
            hu = jnp.dot(n2, wu_ref[ex], preferred_element_type=jnp.float32)
            acts.append((hg * jax.nn.sigmoid(hg) * hu * c).astype(jnp.bfloat16))
        y = y + jnp.dot(jnp.concatenate(acts, axis=1), wd_ref[g], preferred_element_type=jnp.float32)
    o_ref[...] = y


def _moe(n2, comb, h1, w_g, w_u, w_d, tm):
    T = n2.shape[0]
    E, F = EXPERTS_PER_GROUP, EXPERT_FF
    row = lambda i: (i, 0)
    whole = lambda i: (0, 0, 0)
    resident = dict(pipeline_mode=pl.Buffered(1))
    return pl.pallas_call(
        _moe_kernel,
        out_shape=jax.ShapeDtypeStruct((T, D_MODEL), jnp.float32),
        grid=(T // tm,),
        in_specs=[pl.BlockSpec((tm, D_MODEL), row),
                  pl.BlockSpec((tm, ROUTER_W), row),
                  pl.BlockSpec((tm, D_MODEL), row),
                  pl.BlockSpec((N_EXPERTS, D_MODEL, F), whole, **resident),
                  pl.BlockSpec((N_EXPERTS, D_MODEL, F), whole, **resident),
                  pl.BlockSpec((N_GROUPS, E * F, D_MODEL), whole, **resident)],
        out_specs=pl.BlockSpec((tm, D_MODEL), row),
        compiler_params=_cparams(("parallel",)),
        name="moe",
    )(n2, comb, h1, w_g, w_u, w_d)


def _relayout_w_in(w, dtype):
    n_head = C_WIDX + IDX_HEADS
    wh = w[:, :n_head].astype(dtype)
    pad = jnp.zeros((w.shape[0], C_RQ - n_head), dtype)
    head = jnp.concatenate(
        [wh[:, 0:1024],
         wh[:, 1280:1792],
         wh[:, 1024:1280],
         wh[:, 1792:1864],
         pad], axis=1)
    return head, w[:, n_head:].astype(dtype)


def _layer(h, norm1_g, w_in, kv_norm_g, w_kv_up, q_norm_g, k_norm_g, idx_k_norm_g,
           ret_norm_g, w_att_branch, w_ret_branch, w_out, norm2_g, w_group_router,
           b_group_router, w_expert_router, b_expert_router, w_exp_gate, w_exp_up, w_exp_down):
    B, S, D = h.shape
    assert D == D_MODEL and S % KEY_CHUNK == 0 and S % Q_BLOCK == 0 and S <= 65536
    T = B * S
    n_sel = min(TOPK_MAX, S // 4)
    tm = 512 if S % 512 == 0 else 256
    bf = jnp.bfloat16
    f32 = jnp.float32

    x2 = h.reshape(T, D)
    w_head, w_tail = _relayout_w_in(w_in, bf)
    proj = _proj(x2, norm1_g.reshape(1, D).astype(f32), w_head, w_tail, tm, 1024)

    idx_g_pad = jnp.concatenate([idx_k_norm_g.astype(f32), jnp.zeros((LANES - IDX_DIM,), f32)])
    qa, ka, vt, kd, widx = _prep(
        proj, q_norm_g.reshape(1, -1).astype(f32), kv_norm_g.reshape(1, -1).astype(f32),
        w_kv_up.astype(bf), k_norm_g.reshape(1, -1).astype(f32), idx_g_pad.reshape(1, LANES), tm, S)

    att = _attn(qa, proj, widx, kd, ka, vt, B, S, n_sel)
    gret = _ret(proj, ret_norm_g.reshape(1, -1).astype(f32), B, S)

    w_router = jnp.concatenate(
        [w_group_router, w_expert_router,
         jnp.zeros((D, ROUTER_W - N_GROUPS - N_EXPERTS), w_group_router.dtype)], axis=1).astype(bf)
    b_router = jnp.concatenate(
        [b_group_router, b_expert_router,
         jnp.zeros((ROUTER_W - N_GROUPS - N_EXPERTS,), b_group_router.dtype)]).reshape(1, ROUTER_W)
    h1, n2, comb = _merge(att, gret, proj, x2, w_att_branch.astype(bf), w_ret_branch.astype(bf),
                          w_out.astype(bf), norm2_g.reshape(1, D).astype(f32),
                          w_router, b_router.astype(f32), tm)

    w_d = w_exp_down.astype(bf).reshape(N_GROUPS, EXPERTS_PER_GROUP * EXPERT_FF, D)
    out = _moe(n2, comb, h1, w_exp_gate.astype(bf), w_exp_up.astype(bf), w_d, tm)
    return out.reshape(B, S, D)


def kernel(x, norm1_g, w_in, kv_norm_g, w_kv_up, q_norm_g, k_norm_g, idx_k_norm_g, ret_norm_g,
           w_att_branch, w_ret_branch, w_out, norm2_g, w_group_router, b_group_router,
           w_expert_router, b_expert_router, w_exp_gate, w_exp_up, w_exp_down):
    h = x
    for l in range(norm1_g.shape[0]):
        h = _layer(h, norm1_g[l], w_in[l], kv_norm_g[l], w_kv_up[l], q_norm_g[l], k_norm_g[l],
                   idx_k_norm_g[l], ret_norm_g[l], w_att_branch[l], w_ret_branch[l], w_out[l],
                   norm2_g[l], w_group_router[l], b_group_router[l], w_expert_router[l],
                   b_expert_router[l], w_exp_gate[l], w_exp_up[l], w_exp_down[l])
    return h
```
